```python
import math
import jax, jax.numpy as jnp
from jax import lax
import numpy as np

D_MODEL = 1024
BATCH = 8
SEQ = 4096
DEPTH = 1

CHUNK = 64
LEFT_CHUNKS = 8
BAND = LEFT_CHUNKS + 1
D_CONV = D_MODEL // 2
CONV_WIDTH = 31
N_HEADS = 8
HEAD_DIM = 64
D_ATTN = N_HEADS * HEAD_DIM
MAX_REL = 128
N_REL = 2 * MAX_REL + 1
LN_EPS = 1e-5
DEEPNORM_ALPHA = (2.0 * DEPTH) ** 0.25
DEEPNORM_BETA = (8.0 * DEPTH) ** -0.25

COL_SPLITS = [
    D_CONV,
    D_CONV,
    D_CONV,
    D_ATTN,
    D_ATTN,
    D_ATTN,
    D_ATTN,
    D_MODEL,
    D_MODEL,
]
D_IN = sum(COL_SPLITS)

kernel_name = "hybrid_conformer_conv_chunk_attn_deepnorm"


def _layer_norm(x, g, b):
    xf = x.astype(jnp.float32)
    mu = jnp.mean(xf, axis=-1, keepdims=True)
    var = jnp.mean(jnp.square(xf - mu), axis=-1, keepdims=True)
    y = (xf - mu) * lax.rsqrt(var + LN_EPS) * g.astype(jnp.float32) + b.astype(jnp.float32)
    return y.astype(x.dtype)


def _causal_depthwise_conv(u, w, b):
    c = u.shape[-1]
    y = lax.conv_general_dilated(
        u, w.reshape(CONV_WIDTH, 1, c).astype(u.dtype),
        window_strides=(1,), padding=((CONV_WIDTH - 1, 0),),
        dimension_numbers=("NWC", "WIO", "NWC"), feature_group_count=c)
    return y + b


def _chunked_attention(q, k, v, rel_bias):
    bsz, seq, _ = q.shape
    n_chunks = seq // CHUNK
    q = q.reshape(bsz, n_chunks, CHUNK, N_HEADS, HEAD_DIM)
    k = k.reshape(bsz, n_chunks, CHUNK, N_HEADS, HEAD_DIM)
    v = v.reshape(bsz, n_chunks, CHUNK, N_HEADS, HEAD_DIM)
    pad = ((0, 0), (LEFT_CHUNKS, 0), (0, 0), (0, 0), (0, 0))
    kp = jnp.pad(k, pad)
    vp = jnp.pad(v, pad)
    kb = jnp.concatenate([kp[:, w:w + n_chunks] for w in range(BAND)], axis=2)
    vb = jnp.concatenate([vp[:, w:w + n_chunks] for w in range(BAND)], axis=2)

    scale = 1.0 / math.sqrt(HEAD_DIM)
    s = jnp.einsum("bnqhd,bnkhd->bhnqk", q, kb).astype(jnp.float32) * scale

    qi = np.arange(CHUNK)[:, None]
    kj = np.arange(BAND * CHUNK)[None, :]
    rel = np.clip(LEFT_CHUNKS * CHUNK + qi - kj, -MAX_REL, MAX_REL) + MAX_REL
    bias = rel_bias.astype(jnp.float32)[:, rel]
    s = s + bias[:, None]

    key_chunk = np.arange(n_chunks)[:, None] - LEFT_CHUNKS + (np.arange(BAND * CHUNK) // CHUNK)[None, :]
    valid = jnp.asarray(key_chunk >= 0)
    s = jnp.where(valid[:, None, :], s, -1e30)
    p = jax.nn.softmax(s, axis=-1).astype(v.dtype)
    o = jnp.einsum("bhnqk,bnkhd->bnqhd", p, vb)
    return o.reshape(bsz, seq, D_ATTN)


def setup_inputs(seed: int = 0) -> dict:
    key = jax.random.key(seed)
    ks = jax.random.split(key, 16)
    f32 = jnp.float32
    x = jax.random.normal(ks[0], (BATCH, SEQ, D_MODEL), f32)

    w_in = jax.random.normal(ks[1], (D_MODEL, D_IN), f32) * D_MODEL ** -0.5
    v_start = 3 * D_CONV + 2 * D_ATTN
    col_scale = jnp.ones((D_IN,), f32).at[v_start:v_start + D_ATTN].set(DEEPNORM_BETA)
    w_in = w_in * col_scale
    b_in = jax.random.normal(ks[2], (D_IN,), f32) * 0.02

    conv_w = jax.random.normal(ks[3], (CONV_WIDTH, D_CONV), f32) * CONV_WIDTH ** -0.5
    conv_b = jax.random.normal(ks[4], (D_CONV,), f32) * 0.02
    conv_ln_g = 1.0 + 0.05 * jax.random.normal(ks[5], (D_CONV,), f32)
    conv_ln_b = 0.02 * jax.random.normal(ks[6], (D_CONV,), f32)
    w_conv_out = jax.random.normal(ks[7], (D_CONV, D_MODEL), f32) * D_CONV ** -0.5 * DEEPNORM_BETA

    rel_bias = 0.2 * jax.random.normal(ks[8], (N_HEADS, N_REL), f32)
    w_attn_out = jax.random.normal(ks[9], (D_ATTN, D_MODEL), f32) * D_ATTN ** -0.5 * DEEPNORM_BETA

    w_o = jax.random.normal(ks[10], (D_MODEL, D_MODEL), f32) * D_MODEL ** -0.5 * DEEPNORM_BETA
    b_o = 0.02 * jax.random.normal(ks[11], (D_MODEL,), f32)
    out_ln_g = 1.0 + 0.05 * jax.random.normal(ks[12], (D_MODEL,), f32)
    out_ln_b = 0.02 * jax.random.normal(ks[13], (D_MODEL,), f32)
    return {
        "x": x, "w_in": w_in, "b_in": b_in,
        "conv_w": conv_w, "conv_b": conv_b, "conv_ln_g": conv_ln_g, "conv_ln_b": conv_ln_b,
        "w_conv_out": w_conv_out, "rel_bias": rel_bias, "w_attn_out": w_attn_out,
        "w_o": w_o, "b_o": b_o, "out_ln_g": out_ln_g, "out_ln_b": out_ln_b,
    }


def reference(x, w_in, b_in, conv_w, conv_b, conv_ln_g, conv_ln_b, w_conv_out,
              rel_bias, w_attn_out, w_o, b_o, out_ln_g, out_ln_b):
    offsets = np.cumsum(COL_SPLITS)[:-1].tolist()
    for _ in range(DEPTH):
        z = jnp.einsum("bsd,de->bse", x, w_in) + b_in
        (c_val, c_glu, c_gate, q, k, v, a_gate, g_conv, g_attn) = jnp.split(z, offsets, axis=-1)

        u = c_val * jax.nn.sigmoid(c_glu)
        u = _causal_depthwise_conv(u, conv_w, conv_b)
        u = jax.nn.silu(_layer_norm(u, conv_ln_g, conv_ln_b))
        conv_out = jnp.einsum("bsc,cd->bsd", u * jax.nn.silu(c_gate), w_conv_out)

        o = _chunked_attention(q, k, v, rel_bias)
        attn_out = jnp.einsum("bsc,cd->bsd", o * jax.nn.silu(a_gate), w_attn_out)

        h = jax.nn.sigmoid(g_conv) * conv_out + jax.nn.sigmoid(g_attn) * attn_out
        y = jnp.einsum("bsd,de->bse", h, w_o) + b_o

        x = _layer_norm(DEEPNORM_ALPHA * x + y, out_ln_g, out_ln_b)
    return x
```

```python
import functools
import math

import numpy as np
import jax
import jax.numpy as jnp
from jax import lax
from jax.experimental import pallas as pl
from jax.experimental.pallas import tpu as pltpu

D_MODEL = 1024
CHUNK = 64
LEFT_CHUNKS = 8
D_CONV = D_MODEL // 2
CONV_WIDTH = 31
N_HEADS = 8
HEAD_DIM = 64
D_ATTN = N_HEADS * HEAD_DIM
MAX_REL = 128
LN_EPS = 1e-5
DEPTH = 1
DEEPNORM_ALPHA = (2.0 * DEPTH) ** 0.25

C_VAL, C_GLU, C_GATE = 0, D_CONV, 2 * D_CONV
Q_OFF = 3 * D_CONV
K_OFF = Q_OFF + D_ATTN
V_OFF = K_OFF + D_ATTN
AG_OFF = V_OFF + D_ATTN
GC_OFF = AG_OFF + D_ATTN
GA_OFF = GC_OFF + D_MODEL
D_IN = GA_OFF + D_MODEL

LANES = 128
TILE = 512
GROUP = 4 * CHUNK
HIST = LEFT_CHUNKS * CHUNK
KWIN = HIST + GROUP
CONV_PAD = 32
CONV_ROWS = 32
NEG = -1e30
VMEM_LIMIT_BYTES = 56 * 1024 * 1024

assert TILE == HIST and TILE % GROUP == 0 and CONV_PAD >= CONV_WIDTH - 1


def _sigmoid(x):
    return jax.nn.sigmoid(x)


def _silu(x):
    return x * jax.nn.sigmoid(x)


def _layer_kernel(x_ref, w_in_ref, b_in_ref, conv_w_ref, conv_b_ref, cln_g_ref, cln_b_ref,
                  w_co_ref, bias_ref, w_ao_ref, w_o_ref, b_o_ref, oln_g_ref, oln_b_ref,
                  out_ref,
                  u_hist, cg_s, a_conv, h_s, q_s, k_hist, v_hist, ag_s, o_s):
    f32, bf16 = jnp.float32, jnp.bfloat16
    s_idx = pl.program_id(1)

    @pl.when(s_idx == 0)
    def _():
        u_hist[0:CONV_PAD, :] = jnp.zeros((CONV_PAD, D_CONV), f32)
        k_hist[0:HIST, :] = jnp.zeros((HIST, 2 * D_ATTN), bf16)
        v_hist[0:HIST, :] = jnp.zeros((HIST, 2 * D_ATTN), bf16)

    xb = x_ref[...].astype(bf16)

    def proj(lo, width):
        z = jnp.dot(xb, w_in_ref[:, lo:lo + width], preferred_element_type=f32)
        return z + b_in_ref[:, lo:lo + width]

    u_hist[CONV_PAD:CONV_PAD + TILE, :] = proj(C_VAL, D_CONV) * _sigmoid(proj(C_GLU, D_CONV))
    cg_s[...] = _silu(proj(C_GATE, D_CONV))

    conv_b = conv_b_ref[...]
    cln_g = cln_g_ref[...]
    cln_b = cln_b_ref[...]
    row0 = CONV_PAD - (CONV_WIDTH - 1)
    for rb in range(TILE // CONV_ROWS):
        r0 = rb * CONV_ROWS
        acc = jnp.broadcast_to(conv_b, (CONV_ROWS, D_CONV))
        for k in range(CONV_WIDTH):
            acc = acc + conv_w_ref[k:k + 1, :] * u_hist[row0 + r0 + k:row0 + r0 + k + CONV_ROWS, :]
        mu = jnp.mean(acc, axis=-1, keepdims=True)
        cen = acc - mu
        var = jnp.mean(cen * cen, axis=-1, keepdims=True)
        y = cen * lax.rsqrt(var + LN_EPS) * cln_g + cln_b
        a_conv[r0:r0 + CONV_ROWS, :] = (_silu(y) * cg_s[r0:r0 + CONV_ROWS, :]).astype(bf16)

    conv_out = jnp.dot(a_conv[...], w_co_ref[...], preferred_element_type=f32)
    h_s[...] = _sigmoid(proj(GC_OFF, D_MODEL)) * conv_out

    q_s[...] = (proj(Q_OFF, D_ATTN) * (1.0 / math.sqrt(HEAD_DIM))).astype(bf16)
    ag_s[...] = _silu(proj(AG_OFF, D_ATTN))

    lane = lax.broadcasted_iota(jnp.int32, (TILE, LANES), 1)
    low = lane < HEAD_DIM
    for off, hist in ((K_OFF, k_hist), (V_OFF, v_hist)):
        kv = proj(off, D_ATTN)
        for j in range(N_HEADS // 2):
            pair = kv[:, LANES * j:LANES * (j + 1)]
            hist[HIST:HIST + TILE, 2 * LANES * j:2 * LANES * j + LANES] = (
                jnp.where(low, pair, 0.0).astype(bf16))
            hist[HIST:HIST + TILE, 2 * LANES * j + LANES:2 * LANES * (j + 1)] = (
                jnp.where(low, 0.0, pair).astype(bf16))

    first = s_idx == 0
    off_first = jnp.where(first, jnp.float32(-NEG), jnp.float32(0.0))
    lane_g = lax.broadcasted_iota(jnp.int32, (GROUP, LANES), 1)
    low_g = lane_g < HEAD_DIM
    for g in range(TILE // GROUP):
        rows = slice(GROUP * g, GROUP * (g + 1))
        win = slice(GROUP * g, GROUP * g + KWIN)
        n_pre = (HIST - GROUP * g) // GROUP
        offs = [off_first if t < n_pre else None for t in range(KWIN // GROUP)]
        for j in range(N_HEADS // 2):
            qp = q_s[rows, LANES * j:LANES * (j + 1)]
            o_pair = None
            inv_l = []
            for hh in range(2):
                h = 2 * j + hh
                kh = k_hist[win, LANES * h:LANES * (h + 1)]
                s = lax.dot_general(qp, kh, (((1,), (1,)), ((), ())),
                                    preferred_element_type=f32)
                s = s + bias_ref[h]
                slabs = [s[:, GROUP * t:GROUP * (t + 1)] for t in range(KWIN // GROUP)]
                tops = []
                for t, sl in enumerate(slabs):
                    top = jnp.maximum(sl[:, :LANES], sl[:, LANES:])
                    if offs[t] is not None:
                        top = top - offs[t]
                    tops.append(top)
                m = jnp.max(functools.reduce(jnp.maximum, tops), axis=-1, keepdims=True)
                es = []
                for t, sl in enumerate(slabs):
                    mt = m if offs[t] is None else m + offs[t]
                    es.append(jnp.exp(sl - mt))
                e = jnp.concatenate(es, axis=-1)
                inv_l.append(1.0 / jnp.sum(e, axis=-1, keepdims=True))
                vh = v_hist[win, LANES * h:LANES * (h + 1)]
                oh = jnp.dot(e.astype(bf16), vh, preferred_element_type=f32)
                o_pair = oh if o_pair is None else o_pair + oh
            scale = jnp.where(low_g, inv_l[0], inv_l[1])
            o_s[rows, LANES * j:LANES * (j + 1)] = (
                o_pair * scale * ag_s[rows, LANES * j:LANES * (j + 1)]).astype(bf16)

    attn_out = jnp.dot(o_s[...], w_ao_ref[...], preferred_element_type=f32)
    h = h_s[...] + _sigmoid(proj(GA_OFF, D_MODEL)) * attn_out

    y = jnp.dot(h.astype(bf16), w_o_ref[...], preferred_element_type=f32) + b_o_ref[...]
    r = DEEPNORM_ALPHA * x_ref[...] + y
    mu = jnp.mean(r, axis=-1, keepdims=True)
    cen = r - mu
    var = jnp.mean(cen * cen, axis=-1, keepdims=True)
    out_ref[...] = cen * lax.rsqrt(var + LN_EPS) * oln_g_ref[...] + oln_b_ref[...]

    u_hist[0:CONV_PAD, :] = u_hist[TILE:TILE + CONV_PAD, :]
    k_hist[0:HIST, :] = k_hist[TILE:TILE + HIST, :]
    v_hist[0:HIST, :] = v_hist[TILE:TILE + HIST, :]


def _rel_bias_table(rel_bias):
    r = np.arange(GROUP)[:, None]
    c = np.arange(KWIN)[None, :]
    band = c // CHUNK - r // CHUNK
    in_band = (band >= 0) & (band <= LEFT_CHUNKS)
    rel = np.clip(HIST + r - c, -MAX_REL, MAX_REL) + MAX_REL
    table = rel_bias.astype(jnp.float32)[:, rel]
    return jnp.where(jnp.asarray(in_band)[None], table, NEG)


def kernel(x, w_in, b_in, conv_w, conv_b, conv_ln_g, conv_ln_b, w_conv_out, rel_bias,
           w_attn_out, w_o, b_o, out_ln_g, out_ln_b):
    bsz, seq, d_model = x.shape
    assert d_model == D_MODEL and seq % TILE == 0 and w_in.shape == (D_MODEL, D_IN)
    f32, bf16 = jnp.float32, jnp.bfloat16

    def row(v):
        return v.astype(f32).reshape(1, -1)

    def full(shape):
        return pl.BlockSpec(shape, lambda b, s: (0,) * len(shape))

    operands = [
        (x, pl.BlockSpec((None, TILE, D_MODEL), lambda b, s: (b, s, 0))),
        (w_in.astype(bf16), full((D_MODEL, D_IN))),
        (row(b_in), full((1, D_IN))),
        (conv_w.astype(f32), full((CONV_WIDTH, D_CONV))),
        (row(conv_b), full((1, D_CONV))),
        (row(conv_ln_g), full((1, D_CONV))),
        (row(conv_ln_b), full((1, D_CONV))),
        (w_conv_out.astype(bf16), full((D_CONV, D_MODEL))),
        (_rel_bias_table(rel_bias), full((N_HEADS, GROUP, KWIN))),
        (w_attn_out.astype(bf16), full((D_ATTN, D_MODEL))),
        (w_o.astype(bf16), full((D_MODEL, D_MODEL))),
        (row(b_o), full((1, D_MODEL))),
        (row(out_ln_g), full((1, D_MODEL))),
        (row(out_ln_b), full((1, D_MODEL))),
    ]
    scratch = [
        pltpu.VMEM((CONV_PAD + TILE, D_CONV), f32),
        pltpu.VMEM((TILE, D_CONV), f32),
        pltpu.VMEM((TILE, D_CONV), bf16),
        pltpu.VMEM((TILE, D_MODEL), f32),
        pltpu.VMEM((TILE, D_ATTN), bf16),
        pltpu.VMEM((HIST + TILE, 2 * D_ATTN), bf16),
        pltpu.VMEM((HIST + TILE, 2 * D_ATTN), bf16),
        pltpu.VMEM((TILE, D_ATTN), f32),
        pltpu.VMEM((TILE, D_ATTN), bf16),
    ]
    return pl.pallas_call(
        _layer_kernel,
        grid=(bsz, seq // TILE),
        in_specs=[spec for _, spec in operands],
        out_specs=pl.BlockSpec((None, TILE, D_MODEL), lambda b, s: (b, s, 0)),
        out_shape=jax.ShapeDtypeStruct(x.shape, x.dtype),
        scratch_shapes=scratch,
        compiler_params=pltpu.CompilerParams(
            dimension_semantics=("arbitrary", "arbitrary"),
            vmem_limit_bytes=VMEM_LIMIT_BYTES),
        name="hybrid_layer",
    )(*[a for a, _ in operands])
```

```python
import functools
import math

import numpy as np
import jax
import jax.numpy as jnp
from jax import lax
from jax.experimental import pallas as pl
from jax.experimental.pallas import tpu as pltpu

D_MODEL = 1024
CHUNK = 64
LEFT_CHUNKS = 8
D_CONV = D_MODEL // 2
CONV_WIDTH = 31
N_HEADS = 8
HEAD_DIM = 64
D_ATTN = N_HEADS * HEAD_DIM
MAX_REL = 128
LN_EPS = 1e-5
DEPTH = 1
DEEPNORM_ALPHA = (2.0 * DEPTH) ** 0.25

C_VAL, C_GLU, C_GATE = 0, D_CONV, 2 * D_CONV
Q_OFF = 3 * D_CONV
K_OFF = Q_OFF + D_ATTN
V_OFF = K_OFF + D_ATTN
AG_OFF = V_OFF + D_ATTN
GC_OFF = AG_OFF + D_ATTN
GA_OFF = GC_OFF + D_MODEL
D_IN = GA_OFF + D_MODEL

LANES = 128
SUBLANES = 8
MXU_COLS = 256
TILE = 512
GROUP = 4 * CHUNK
HIST = LEFT_CHUNKS * CHUNK
KWIN = HIST + GROUP
CONV_PAD = 32
CONV_ROWS = 32
U_ROWS = CONV_PAD + TILE
CONV_SLABS = D_CONV // LANES
ROW_PHASES = 2
NEG = -1e30
VMEM_LIMIT_BYTES = 58 * 1024 * 1024

assert TILE == HIST and TILE % GROUP == 0 and CONV_PAD >= CONV_WIDTH - 1
assert CONV_PAD % SUBLANES == 0 and CONV_ROWS % (ROW_PHASES * SUBLANES) == 0


def _sigmoid(x):
    return jax.nn.sigmoid(x)


def _silu(x):
    return x * jax.nn.sigmoid(x)


def _layer_kernel(x_ref, w_in_ref, b_in_ref, conv_w_ref, conv_b_ref, cln_g_ref, cln_b_ref,
                  w_co_ref, bias_ref, w_ao_ref, w_o_ref, b_o_ref, oln_g_ref, oln_b_ref,
                  out_ref,
                  xb_s, u_hist, cg_s, a_conv, gc_s, ga_s, q_s, k_hist, v_hist, ag_s, o_s):
    f32, bf16 = jnp.float32, jnp.bfloat16
    s_idx = pl.program_id(1)

    @pl.when(s_idx == 0)
    def _():
        u_hist[:, 0:CONV_PAD, :] = jnp.zeros((CONV_SLABS, CONV_PAD, LANES), f32)
        k_hist[0:HIST, :] = jnp.zeros((HIST, 2 * D_ATTN), bf16)
        v_hist[0:HIST, :] = jnp.zeros((HIST, 2 * D_ATTN), bf16)

    xb_s[...] = x_ref[...].astype(bf16)

    def proj(lo, width):
        z = jnp.dot(xb_s[...], w_in_ref[:, lo:lo + width], preferred_element_type=f32)
        return z + b_in_ref[:, lo:lo + width]

    u = proj(C_VAL, D_CONV) * _sigmoid(proj(C_GLU, D_CONV))
    cg = _silu(proj(C_GATE, D_CONV))
    for c in range(CONV_SLABS):
        u_hist[c, CONV_PAD:U_ROWS, :] = u[:, LANES * c:LANES * (c + 1)]
        cg_s[c] = cg[:, LANES * c:LANES * (c + 1)]

    lane = lax.broadcasted_iota(jnp.int32, (TILE, LANES), 1)
    low = lane < HEAD_DIM

    def q_piece(c):
        q_s[:, c:c + MXU_COLS] = (
            proj(Q_OFF + c, MXU_COLS) * (1.0 / math.sqrt(HEAD_DIM))).astype(bf16)

    def kv_piece(off, hist, c):
        kv = proj(off + c, MXU_COLS)
        for jj in range(MXU_COLS // LANES):
            pair = kv[:, LANES * jj:LANES * (jj + 1)]
            base = 2 * (c + LANES * jj)
            hist[HIST:HIST + TILE, base:base + LANES] = jnp.where(low, pair, 0.0).astype(bf16)
            hist[HIST:HIST + TILE, base + LANES:base + 2 * LANES] = (
                jnp.where(low, 0.0, pair).astype(bf16))

    def gate_piece(off, dst, act, c):
        dst[:, c:c + MXU_COLS] = act(proj(off + c, MXU_COLS)).astype(bf16)

    pieces = []
    for c in range(0, D_ATTN, MXU_COLS):
        pieces.append(functools.partial(q_piece, c))
    for c in range(0, D_ATTN, MXU_COLS):
        pieces.append(functools.partial(kv_piece, K_OFF, k_hist, c))
    for c in range(0, D_ATTN, MXU_COLS):
        pieces.append(functools.partial(kv_piece, V_OFF, v_hist, c))
    for c in range(0, D_ATTN, MXU_COLS):
        pieces.append(functools.partial(gate_piece, AG_OFF, ag_s, _silu, c))
    for c in range(0, D_MODEL, MXU_COLS):
        pieces.append(functools.partial(gate_piece, GC_OFF, gc_s, _sigmoid, c))
    for c in range(0, D_MODEL, MXU_COLS):
        pieces.append(functools.partial(gate_piece, GA_OFF, ga_s, _sigmoid, c))
    n_blocks = TILE // CONV_ROWS
    assert len(pieces) == n_blocks

    cln_g = cln_g_ref[...]
    cln_b = cln_b_ref[...]
    row0 = CONV_PAD - (CONV_WIDTH - 1)
    span = ROW_PHASES * SUBLANES
    for rb in range(n_blocks):
        pieces[rb]()
        r0 = rb * CONV_ROWS
        sets = [(a, p) for a in range(CONV_ROWS // span) for p in range(ROW_PHASES)]

        def rows_of(a, p, shift=0):
            return pl.ds(shift + r0 + span * a + p, SUBLANES, stride=ROW_PHASES)

        acc = [[jnp.broadcast_to(conv_b_ref[:, LANES * c:LANES * (c + 1)], (SUBLANES, LANES))
                for c in range(CONV_SLABS)] for _ in sets]
        for k in range(CONV_WIDTH):
            for c in range(CONV_SLABS):
                w_kc = conv_w_ref[k:k + 1, LANES * c:LANES * (c + 1)]
                for i, (a, p) in enumerate(sets):
                    acc[i][c] = acc[i][c] + w_kc * u_hist[c, rows_of(a, p, row0 + k), :]
        conv = jnp.concatenate([jnp.concatenate(acc_i, axis=1) for acc_i in acc], axis=0)
        gate = jnp.concatenate(
            [jnp.concatenate([cg_s[c, rows_of(a, p), :] for c in range(CONV_SLABS)], axis=1)
             for a, p in sets], axis=0)
        mu = jnp.mean(conv, axis=-1, keepdims=True)
        cen = conv - mu
        var = jnp.mean(cen * cen, axis=-1, keepdims=True)
        y = cen * lax.rsqrt(var + LN_EPS) * cln_g + cln_b
        act = _silu(y) * gate
        for i, (a, p) in enumerate(sets):
            for c in range(CONV_SLABS):
                a_conv[c, rows_of(a, p), :] = act[SUBLANES * i:SUBLANES * (i + 1),
                                                  LANES * c:LANES * (c + 1)]

    first = s_idx == 0
    off_first = jnp.where(first, jnp.float32(-NEG), jnp.float32(0.0))
    lane_g = lax.broadcasted_iota(jnp.int32, (GROUP, LANES), 1)
    low_g = lane_g < HEAD_DIM
    for g in range(TILE // GROUP):
        rows = slice(GROUP * g, GROUP * (g + 1))
        win = slice(GROUP * g, GROUP * g + KWIN)
        n_pre = (HIST - GROUP * g) // GROUP
        offs = [off_first if t < n_pre else None for t in range(KWIN // GROUP)]
        for j in range(N_HEADS // 2):
            qp = q_s[rows, LANES * j:LANES * (j + 1)]
            o_pair = None
            inv_l = []
            for hh in range(2):
                h = 2 * j + hh
                kh = k_hist[win, LANES * h:LANES * (h + 1)]
                s = lax.dot_general(qp, kh, (((1,), (1,)), ((), ())),
                                    preferred_element_type=f32)
                s = s + bias_ref[h]
                slabs = [s[:, GROUP * t:GROUP * (t + 1)] for t in range(KWIN // GROUP)]
                tops = []
                for t, sl in enumerate(slabs):
                    top = jnp.maximum(sl[:, :LANES], sl[:, LANES:])
                    if offs[t] is not None:
                        top = top - offs[t]
                    tops.append(top)
                m = jnp.max(functools.reduce(jnp.maximum, tops), axis=-1, keepdims=True)
                es = []
                for t, sl in enumerate(slabs):
                    mt = m if offs[t] is None else m + offs[t]
                    es.append(jnp.exp(sl - mt))
                e = jnp.concatenate(es, axis=-1)
                inv_l.append(1.0 / jnp.sum(e, axis=-1, keepdims=True))
                vh = v_hist[win, LANES * h:LANES * (h + 1)]
                oh = jnp.dot(e.astype(bf16), vh, preferred_element_type=f32)
                o_pair = oh if o_pair is None else o_pair + oh
            scale = jnp.where(low_g, inv_l[0], inv_l[1])
            o_s[rows, LANES * j:LANES * (j + 1)] = (
                o_pair * scale * ag_s[rows, LANES * j:LANES * (j + 1)]).astype(bf16)

    a_all = jnp.concatenate([a_conv[c] for c in range(CONV_SLABS)], axis=1).astype(bf16)
    conv_out = jnp.dot(a_all, w_co_ref[...], preferred_element_type=f32)
    attn_out = jnp.dot(o_s[...], w_ao_ref[...], preferred_element_type=f32)
    h = gc_s[...] * conv_out + ga_s[...] * attn_out
    y = jnp.dot(h.astype(bf16), w_o_ref[...], preferred_element_type=f32) + b_o_ref[...]
    r = DEEPNORM_ALPHA * x_ref[...] + y
    mu = jnp.mean(r, axis=-1, keepdims=True)
    cen = r - mu
    var = jnp.mean(cen * cen, axis=-1, keepdims=True)
    out_ref[...] = cen * lax.rsqrt(var + LN_EPS) * oln_g_ref[...] + oln_b_ref[...]

    u_hist[:, 0:CONV_PAD, :] = u_hist[:, TILE:U_ROWS, :]
    k_hist[0:HIST, :] = k_hist[TILE:TILE + HIST, :]
    v_hist[0:HIST, :] = v_hist[TILE:TILE + HIST, :]


def _rel_bias_table(rel_bias):
    n = GROUP + KWIN
    diag = np.arange(n)
    idx = np.clip(HIST + (GROUP - 1) - diag, -MAX_REL, MAX_REL) + MAX_REL
    n_far = int(np.sum(idx == 2 * MAX_REL)) - 1
    n_near = int(np.sum(idx == 0)) - 1
    rb = rel_bias.astype(jnp.float32)
    vec = jnp.concatenate([
        jnp.broadcast_to(rb[:, -1:], (N_HEADS, n_far)),
        rb[:, ::-1],
        jnp.broadcast_to(rb[:, :1], (N_HEADS, n_near))], axis=1)
    assert vec.shape == (N_HEADS, n)
    skew = jnp.tile(vec, (1, GROUP))[:, :GROUP * (n - 1)].reshape(N_HEADS, GROUP, n - 1)
    table = skew[:, :, GROUP - 1:GROUP - 1 + KWIN]
    r = np.arange(GROUP)[:, None]
    c = np.arange(KWIN)[None, :]
    band = c // CHUNK - r // CHUNK
    in_band = (band >= 0) & (band <= LEFT_CHUNKS)
    return jnp.where(jnp.asarray(in_band)[None], table, NEG)


def kernel(x, w_in, b_in, conv_w, conv_b, conv_ln_g, conv_ln_b, w_conv_out, rel_bias,
           w_attn_out, w_o, b_o, out_ln_g, out_ln_b):
    bsz, seq, d_model = x.shape
    assert d_model == D_MODEL and seq % TILE == 0 and w_in.shape == (D_MODEL, D_IN)
    f32, bf16 = jnp.float32, jnp.bfloat16

    def row(v):
        return v.astype(f32).reshape(1, -1)

    def full(shape):
        return pl.BlockSpec(shape, lambda b, s: (0,) * len(shape))

    operands = [
        (x, pl.BlockSpec((None, TILE, D_MODEL), lambda b, s: (b, s, 0))),
        (w_in.astype(bf16), full((D_MODEL, D_IN))),
        (row(b_in), full((1, D_IN))),
        (conv_w.astype(f32), full((CONV_WIDTH, D_CONV))),
        (row(conv_b), full((1, D_CONV))),
        (row(conv_ln_g), full((1, D_CONV))),
        (row(conv_ln_b), full((1, D_CONV))),
        (w_conv_out.astype(bf16), full((D_CONV, D_MODEL))),
        (_rel_bias_table(rel_bias), full((N_HEADS, GROUP, KWIN))),
        (w_attn_out.astype(bf16), full((D_ATTN, D_MODEL))),
        (w_o.astype(bf16), full((D_MODEL, D_MODEL))),
        (row(b_o), full((1, D_MODEL))),
        (row(out_ln_g), full((1, D_MODEL))),
        (row(out_ln_b), full((1, D_MODEL))),
    ]
    scratch = [
        pltpu.VMEM((TILE, D_MODEL), bf16),
        pltpu.VMEM((CONV_SLABS, U_ROWS, LANES), f32),
        pltpu.VMEM((CONV_SLABS, TILE, LANES), f32),
        pltpu.VMEM((CONV_SLABS, TILE, LANES), f32),
        pltpu.VMEM((TILE, D_MODEL), bf16),
        pltpu.VMEM((TILE, D_MODEL), bf16),
        pltpu.VMEM((TILE, D_ATTN), bf16),
        pltpu.VMEM((HIST + TILE, 2 * D_ATTN), bf16),
        pltpu.VMEM((HIST + TILE, 2 * D_ATTN), bf16),
        pltpu.VMEM((TILE, D_ATTN), bf16),
        pltpu.VMEM((TILE, D_ATTN), bf16),
    ]
    return pl.pallas_call(
        _layer_kernel,
        grid=(bsz, seq // TILE),
        in_specs=[spec for _, spec in operands],
        out_specs=pl.BlockSpec((None, TILE, D_MODEL), lambda b, s: (b, s, 0)),
        out_shape=jax.ShapeDtypeStruct(x.shape, x.dtype),
        scratch_shapes=scratch,
        compiler_params=pltpu.CompilerParams(
            dimension_semantics=("arbitrary", "arbitrary"),
            vmem_limit_bytes=VMEM_LIMIT_BYTES),
        name="hybrid_layer",
    )(*[a for a, _ in operands])
```

```python
import functools
import math

import numpy as np
import jax
import jax.numpy as jnp
from jax import lax
from jax.experimental import pallas as pl
from jax.experimental.pallas import tpu as pltpu

D_MODEL = 1024
CHUNK = 64
LEFT_CHUNKS = 8
D_CONV = D_MODEL // 2
CONV_WIDTH = 31
N_HEADS = 8
HEAD_DIM = 64
D_ATTN = N_HEADS * HEAD_DIM
MAX_REL = 128
LN_EPS = 1e-5
DEPTH = 1
DEEPNORM_ALPHA = (2.0 * DEPTH) ** 0.25

C_VAL, C_GLU, C_GATE = 0, D_CONV, 2 * D_CONV
Q_OFF = 3 * D_CONV
K_OFF = Q_OFF + D_ATTN
V_OFF = K_OFF + D_ATTN
AG_OFF = V_OFF + D_ATTN
GC_OFF = AG_OFF + D_ATTN
GA_OFF = GC_OFF + D_MODEL
D_IN = GA_OFF + D_MODEL

LANES = 128
SUBLANES = 8
MXU_COLS = 256
TILE = 512
GROUP = 4 * CHUNK
HIST = LEFT_CHUNKS * CHUNK
KWIN = HIST + GROUP
CONV_PAD = 32
CONV_ROWS = 32
U_ROWS = CONV_PAD + TILE
CONV_SLABS = D_CONV // LANES
ROW_PHASES = 2
NEG = -1e30
LOG2E = math.log2(math.e)
VMEM_LIMIT_BYTES = 58 * 1024 * 1024

assert TILE == HIST and TILE % GROUP == 0 and CONV_PAD >= CONV_WIDTH - 1
assert CONV_PAD % SUBLANES == 0 and CONV_ROWS % (ROW_PHASES * SUBLANES) == 0


def _sigmoid(x):
    return jax.nn.sigmoid(x)


def _silu(x):
    return x * jax.nn.sigmoid(x)


def _layer_kernel(x_ref, w_in_ref, b_in_ref, conv_w_ref, conv_b_ref, cln_g_ref, cln_b_ref,
                  w_co_ref, bias_ref, w_ao_ref, w_o_ref, b_o_ref, oln_g_ref, oln_b_ref,
                  out_ref,
                  xb_s, u_hist, cg_s, a_conv, gc_s, ga_s, q_s, k_hist, v_hist, ag_s, o_s):
    f32, bf16 = jnp.float32, jnp.bfloat16
    s_idx = pl.program_id(1)

    @pl.when(s_idx == 0)
    def _():
        u_hist[:, 0:CONV_PAD, :] = jnp.zeros((CONV_SLABS, CONV_PAD, LANES), f32)
        k_hist[0:HIST, :] = jnp.zeros((HIST, 2 * D_ATTN), bf16)
        v_hist[0:HIST, :] = jnp.zeros((HIST, 2 * D_ATTN), bf16)

    xb_s[...] = x_ref[...].astype(bf16)

    def proj(lo, width):
        z = jnp.dot(xb_s[...], w_in_ref[:, lo:lo + width], preferred_element_type=f32)
        return z + b_in_ref[:, lo:lo + width]

    u = proj(C_VAL, D_CONV) * _sigmoid(proj(C_GLU, D_CONV))
    cg = _silu(proj(C_GATE, D_CONV))
    for c in range(CONV_SLABS):
        u_hist[c, CONV_PAD:U_ROWS, :] = u[:, LANES * c:LANES * (c + 1)]
        cg_s[c] = cg[:, LANES * c:LANES * (c + 1)]

    lane = lax.broadcasted_iota(jnp.int32, (TILE, LANES), 1)
    low = lane < HEAD_DIM

    def q_piece(c):
        q_s[:, c:c + MXU_COLS] = (
            proj(Q_OFF + c, MXU_COLS) * (LOG2E / math.sqrt(HEAD_DIM))).astype(bf16)

    def kv_piece(off, hist, fill, c):
        kv = proj(off + c, MXU_COLS)
        for jj in range(MXU_COLS // LANES):
            pair = kv[:, LANES * jj:LANES * (jj + 1)]
            base = 2 * (c + LANES * jj)
            hist[HIST:HIST + TILE, base:base + LANES] = jnp.where(low, pair, fill).astype(bf16)
            hist[HIST:HIST + TILE, base + LANES:base + 2 * LANES] = (
                jnp.where(low, fill, pair).astype(bf16))

    def gate_piece(off, dst, act, c):
        dst[:, c:c + MXU_COLS] = act(proj(off + c, MXU_COLS)).astype(bf16)

    pieces = []
    for c in range(0, D_ATTN, MXU_COLS):
        pieces.append(functools.partial(q_piece, c))
    for c in range(0, D_ATTN, MXU_COLS):
        pieces.append(functools.partial(kv_piece, K_OFF, k_hist, 0.0, c))
    for c in range(0, D_ATTN, MXU_COLS):
        pieces.append(functools.partial(kv_piece, V_OFF, v_hist, 1.0, c))
    for c in range(0, D_ATTN, MXU_COLS):
        pieces.append(functools.partial(gate_piece, AG_OFF, ag_s, _silu, c))
    for c in range(0, D_MODEL, MXU_COLS):
        pieces.append(functools.partial(gate_piece, GC_OFF, gc_s, _sigmoid, c))
    for c in range(0, D_MODEL, MXU_COLS):
        pieces.append(functools.partial(gate_piece, GA_OFF, ga_s, _sigmoid, c))
    n_blocks = TILE // CONV_ROWS
    assert len(pieces) == n_blocks

    cln_g = cln_g_ref[...]
    cln_b = cln_b_ref[...]
    row0 = CONV_PAD - (CONV_WIDTH - 1)
    span = ROW_PHASES * SUBLANES
    for rb in range(n_blocks):
        pieces[rb]()
        r0 = rb * CONV_ROWS
        sets = [(a, p) for a in range(CONV_ROWS // span) for p in range(ROW_PHASES)]

        def rows_of(a, p, shift=0):
            return pl.ds(shift + r0 + span * a + p, SUBLANES, stride=ROW_PHASES)

        acc = [[jnp.broadcast_to(conv_b_ref[:, LANES * c:LANES * (c + 1)], (SUBLANES, LANES))
                for c in range(CONV_SLABS)] for _ in sets]
        for k in range(CONV_WIDTH):
            for c in range(CONV_SLABS):
                w_kc = conv_w_ref[k:k + 1, LANES * c:LANES * (c + 1)]
                for i, (a, p) in enumerate(sets):
                    acc[i][c] = acc[i][c] + w_kc * u_hist[c, rows_of(a, p, row0 + k), :]
        conv = jnp.concatenate([jnp.concatenate(acc_i, axis=1) for acc_i in acc], axis=0)
        gate = jnp.concatenate(
            [jnp.concatenate([cg_s[c, rows_of(a, p), :] for c in range(CONV_SLABS)], axis=1)
             for a, p in sets], axis=0)
        mu = jnp.mean(conv, axis=-1, keepdims=True)
        cen = conv - mu
        var = jnp.mean(cen * cen, axis=-1, keepdims=True)
        y = cen * lax.rsqrt(var + LN_EPS) * cln_g + cln_b
        act = _silu(y) * gate
        for i, (a, p) in enumerate(sets):
            for c in range(CONV_SLABS):
                a_conv[c, rows_of(a, p), :] = act[SUBLANES * i:SUBLANES * (i + 1),
                                                  LANES * c:LANES * (c + 1)]

    first = s_idx == 0
    off_first = jnp.where(first, jnp.float32(-NEG), jnp.float32(0.0))
    lane_g = lax.broadcasted_iota(jnp.int32, (GROUP, LANES), 1)
    low_g = lane_g < HEAD_DIM
    col_tiles = KWIN // LANES
    band_tiles = (LEFT_CHUNKS + 1) * CHUNK // LANES + 1
    zero_tile = jnp.zeros((CHUNK, LANES), bf16)
    for g in range(TILE // GROUP):
        rows = slice(GROUP * g, GROUP * (g + 1))
        win = slice(GROUP * g, GROUP * g + KWIN)
        n_pre = (HIST - GROUP * g) // GROUP
        offs = [off_first if t < n_pre else None for t in range(KWIN // GROUP)]
        for j in range(N_HEADS // 2):
            qp = q_s[rows, LANES * j:LANES * (j + 1)]
            o_heads = []
            for hh in range(2):
                h = 2 * j + hh
                kh = k_hist[win, LANES * h:LANES * (h + 1)]
                s = lax.dot_general(qp, kh, (((1,), (1,)), ((), ())),
                                    preferred_element_type=f32)
                e_rows = []
                for ci in range(GROUP // CHUNK):
                    r = slice(CHUNK * ci, CHUNK * (ci + 1))
                    first_tile = (CHUNK * ci) // LANES
                    live = range(first_tile, first_tile + band_tiles)
                    sc = {t: s[r, LANES * t:LANES * (t + 1)] + bias_ref[h, r, LANES * t:LANES * (t + 1)]
                          for t in live}
                    tops = [sc[t] if offs[LANES * t // GROUP] is None
                            else sc[t] - offs[LANES * t // GROUP] for t in live]
                    m = jnp.max(functools.reduce(jnp.maximum, tops), axis=-1, keepdims=True)
                    e_tiles = []
                    for t in range(col_tiles):
                        if t in live:
                            off = offs[LANES * t // GROUP]
                            mt = m if off is None else m + off
                            e_tiles.append(jnp.exp2(sc[t] - mt).astype(bf16))
                        else:
                            e_tiles.append(zero_tile)
                    e_rows.append(jnp.concatenate(e_tiles, axis=1))
                e = jnp.concatenate(e_rows, axis=0)
                vh = v_hist[win, LANES * h:LANES * (h + 1)]
                o_heads.append(jnp.dot(e, vh, preferred_element_type=f32))
            num = jnp.where(low_g, o_heads[0], o_heads[1])
            den = jnp.where(low_g, pltpu.roll(o_heads[0], HEAD_DIM, 1),
                            pltpu.roll(o_heads[1], HEAD_DIM, 1))
            o_s[rows, LANES * j:LANES * (j + 1)] = (
                num / den * ag_s[rows, LANES * j:LANES * (j + 1)]).astype(bf16)

    a_all = jnp.concatenate([a_conv[c] for c in range(CONV_SLABS)], axis=1).astype(bf16)
    conv_out = jnp.dot(a_all, w_co_ref[...], preferred_element_type=f32)
    attn_out = jnp.dot(o_s[...], w_ao_ref[...], preferred_element_type=f32)
    h = gc_s[...] * conv_out.astype(bf16) + ga_s[...] * attn_out.astype(bf16)
    y = jnp.dot(h, w_o_ref[...], preferred_element_type=f32) + b_o_ref[...]
    r = DEEPNORM_ALPHA * x_ref[...] + y
    mu = jnp.mean(r, axis=-1, keepdims=True)
    cen = r - mu
    var = jnp.mean(cen * cen, axis=-1, keepdims=True)
    out_ref[...] = cen * lax.rsqrt(var + LN_EPS) * oln_g_ref[...] + oln_b_ref[...]

    u_hist[:, 0:CONV_PAD, :] = u_hist[:, TILE:U_ROWS, :]
    k_hist[0:HIST, :] = k_hist[TILE:TILE + HIST, :]
    v_hist[0:HIST, :] = v_hist[TILE:TILE + HIST, :]


def _rel_bias_table(rel_bias):
    n = GROUP + KWIN
    diag = np.arange(n)
    idx = np.clip(HIST + (GROUP - 1) - diag, -MAX_REL, MAX_REL) + MAX_REL
    n_far = int(np.sum(idx == 2 * MAX_REL)) - 1
    n_near = int(np.sum(idx == 0)) - 1
    rb = rel_bias.astype(jnp.float32)
    vec = jnp.concatenate([
        jnp.broadcast_to(rb[:, -1:], (N_HEADS, n_far)),
        rb[:, ::-1],
        jnp.broadcast_to(rb[:, :1], (N_HEADS, n_near))], axis=1)
    assert vec.shape == (N_HEADS, n)
    skew = jnp.tile(vec, (1, GROUP))[:, :GROUP * (n - 1)].reshape(N_HEADS, GROUP, n - 1)
    table = skew[:, :, GROUP - 1:GROUP - 1 + KWIN]
    r = np.arange(GROUP)[:, None]
    c = np.arange(KWIN)[None, :]
    band = c // CHUNK - r // CHUNK
    in_band = (band >= 0) & (band <= LEFT_CHUNKS)
    return jnp.where(jnp.asarray(in_band)[None], table * LOG2E, NEG)


def kernel(x, w_in, b_in, conv_w, conv_b, conv_ln_g, conv_ln_b, w_conv_out, rel_bias,
           w_attn_out, w_o, b_o, out_ln_g, out_ln_b):
    bsz, seq, d_model = x.shape
    assert d_model == D_MODEL and seq % TILE == 0 and w_in.shape == (D_MODEL, D_IN)
    f32, bf16 = jnp.float32, jnp.bfloat16

    def row(v):
        return v.astype(f32).reshape(1, -1)

    def full(shape):
        return pl.BlockSpec(shape, lambda b, s: (0,) * len(shape))

    operands = [
        (x, pl.BlockSpec((None, TILE, D_MODEL), lambda b, s: (b, s, 0))),
        (w_in.astype(bf16), full((D_MODEL, D_IN))),
        (row(b_in), full((1, D_IN))),
        (conv_w.astype(f32), full((CONV_WIDTH, D_CONV))),
        (row(conv_b), full((1, D_CONV))),
        (row(conv_ln_g), full((1, D_CONV))),
        (row(conv_ln_b), full((1, D_CONV))),
        (w_conv_out.astype(bf16), full((D_CONV, D_MODEL))),
        (_rel_bias_table(rel_bias), full((N_HEADS, GROUP, KWIN))),
        (w_attn_out.astype(bf16), full((D_ATTN, D_MODEL))),
        (w_o.astype(bf16), full((D_MODEL, D_MODEL))),
        (row(b_o), full((1, D_MODEL))),
        (row(out_ln_g), full((1, D_MODEL))),
        (row(out_ln_b), full((1, D_MODEL))),
    ]
    scratch = [
        pltpu.VMEM((TILE, D_MODEL), bf16),
        pltpu.VMEM((CONV_SLABS, U_ROWS, LANES), f32),
        pltpu.VMEM((CONV_SLABS, TILE, LANES), f32),
        pltpu.VMEM((CONV_SLABS, TILE, LANES), f32),
        pltpu.VMEM((TILE, D_MODEL), bf16),
        pltpu.VMEM((TILE, D_MODEL), bf16),
        pltpu.VMEM((TILE, D_ATTN), bf16),
        pltpu.VMEM((HIST + TILE, 2 * D_ATTN), bf16),
        pltpu.VMEM((HIST + TILE, 2 * D_ATTN), bf16),
        pltpu.VMEM((TILE, D_ATTN), bf16),
        pltpu.VMEM((TILE, D_ATTN), bf16),
    ]
    return pl.pallas_call(
        _layer_kernel,
        grid=(bsz, seq // TILE),
        in_specs=[spec for _, spec in operands],
        out_specs=pl.BlockSpec((None, TILE, D_MODEL), lambda b, s: (b, s, 0)),
        out_shape=jax.ShapeDtypeStruct(x.shape, x.dtype),
        scratch_shapes=scratch,
        compiler_params=pltpu.CompilerParams(
            dimension_semantics=("arbitrary", "arbitrary"),
            vmem_limit_bytes=VMEM_LIMIT_BYTES),
        name="hybrid_layer",
    )(*[a for a, _ in operands])
```

```python
import functools
import math

import numpy as np
import jax
import jax.numpy as jnp
from jax import lax
from jax.experimental import pallas as pl
from jax.experimental.pallas import tpu as pltpu

D_MODEL = 1024
CHUNK = 64
LEFT_CHUNKS = 8
D_CONV = D_MODEL // 2
CONV_WIDTH = 31
N_HEADS = 8
HEAD_DIM = 64
D_ATTN = N_HEADS * HEAD_DIM
MAX_REL = 128
LN_EPS = 1e-5
DEPTH = 1
DEEPNORM_ALPHA = (2.0 * DEPTH) ** 0.25

C_VAL, C_GLU, C_GATE = 0, D_CONV, 2 * D_CONV
Q_OFF = 3 * D_CONV
K_OFF = Q_OFF + D_ATTN
V_OFF = K_OFF + D_ATTN
AG_OFF = V_OFF + D_ATTN
GC_OFF = AG_OFF + D_ATTN
GA_OFF = GC_OFF + D_MODEL
D_IN = GA_OFF + D_MODEL

LANES = 128
SUBLANES = 8
MXU_COLS = 256
TILE = 512
GROUP = 4 * CHUNK
HIST = LEFT_CHUNKS * CHUNK
KWIN = HIST + GROUP
CONV_PAD = 32
CONV_ROWS = 32
U_ROWS = CONV_PAD + TILE
CONV_SLABS = D_CONV // LANES
ROW_PHASES = 2
NEG = -1e30
LOG2E = math.log2(math.e)
VMEM_LIMIT_BYTES = 58 * 1024 * 1024

assert TILE == HIST and TILE % GROUP == 0 and CONV_PAD >= CONV_WIDTH - 1
assert CONV_PAD % SUBLANES == 0 and CONV_ROWS % (ROW_PHASES * SUBLANES) == 0


def _sigmoid(x):
    return jax.nn.sigmoid(x)


def _silu(x):
    return x * jax.nn.sigmoid(x)


def _layer_kernel(x_ref, w_in_ref, b_in_ref, conv_w_ref, conv_b_ref, cln_g_ref, cln_b_ref,
                  w_co_ref, bias_ref, w_ao_ref, w_o_ref, b_o_ref, oln_g_ref, oln_b_ref,
                  out_ref,
                  xb_s, u_hist, cg_s, a_conv, gc_s, ga_s, q_s, kt_hist, v_hist, ag_s, o_s):
    f32, bf16 = jnp.float32, jnp.bfloat16
    s_idx = pl.program_id(1)

    @pl.when(s_idx == 0)
    def _():
        u_hist[:, 0:CONV_PAD, :] = jnp.zeros((CONV_SLABS, CONV_PAD, LANES), f32)
        kt_hist[...] = jnp.zeros(kt_hist.shape, bf16)
        v_hist[0:HIST, :] = jnp.zeros((HIST, 2 * D_ATTN), bf16)

    xb_s[...] = x_ref[...].astype(bf16)

    def proj(lo, width):
        z = jnp.dot(xb_s[...], w_in_ref[:, lo:lo + width], preferred_element_type=f32)
        return z + b_in_ref[:, lo:lo + width]

    u = proj(C_VAL, D_CONV) * _sigmoid(proj(C_GLU, D_CONV))
    cg = _silu(proj(C_GATE, D_CONV))
    for c in range(CONV_SLABS):
        u_hist[c, CONV_PAD:U_ROWS, :] = u[:, LANES * c:LANES * (c + 1)]
        cg_s[c] = cg[:, LANES * c:LANES * (c + 1)]

    lane = lax.broadcasted_iota(jnp.int32, (TILE, LANES), 1)
    low = lane < HEAD_DIM

    def q_piece(c):
        q_s[:, c:c + MXU_COLS] = (
            proj(Q_OFF + c, MXU_COLS) * (LOG2E / math.sqrt(HEAD_DIM))).astype(bf16)

    def k_piece(c):
        kv = proj(K_OFF + c, MXU_COLS)
        for jj in range(MXU_COLS // LANES):
            pair_t = kv[:, LANES * jj:LANES * (jj + 1)].T.astype(bf16)
            h0 = 2 * (c // LANES + jj)
            kt_hist[LANES * h0:LANES * h0 + HEAD_DIM, HIST:HIST + TILE] = pair_t[0:HEAD_DIM]
            kt_hist[LANES * (h0 + 1) + HEAD_DIM:LANES * (h0 + 2), HIST:HIST + TILE] = (
                pair_t[HEAD_DIM:LANES])

    def v_piece(c):
        kv = proj(V_OFF + c, MXU_COLS)
        for jj in range(MXU_COLS // LANES):
            pair = kv[:, LANES * jj:LANES * (jj + 1)]
            base = 2 * (c + LANES * jj)
            v_hist[HIST:HIST + TILE, base:base + LANES] = jnp.where(low, pair, 1.0).astype(bf16)
            v_hist[HIST:HIST + TILE, base + LANES:base + 2 * LANES] = (
                jnp.where(low, 1.0, pair).astype(bf16))

    def gate_piece(off, dst, act, c):
        dst[:, c:c + MXU_COLS] = act(proj(off + c, MXU_COLS)).astype(bf16)

    pieces = []
    for c in range(0, D_ATTN, MXU_COLS):
        pieces.append(functools.partial(q_piece, c))
    for c in range(0, D_ATTN, MXU_COLS):
        pieces.append(functools.partial(k_piece, c))
    for c in range(0, D_ATTN, MXU_COLS):
        pieces.append(functools.partial(v_piece, c))
    for c in range(0, D_ATTN, MXU_COLS):
        pieces.append(functools.partial(gate_piece, AG_OFF, ag_s, _silu, c))
    for c in range(0, D_MODEL, MXU_COLS):
        pieces.append(functools.partial(gate_piece, GC_OFF, gc_s, _sigmoid, c))
    for c in range(0, D_MODEL, MXU_COLS):
        pieces.append(functools.partial(gate_piece, GA_OFF, ga_s, _sigmoid, c))
    n_blocks = TILE // CONV_ROWS
    assert len(pieces) == n_blocks

    cln_g = cln_g_ref[...]
    cln_b = cln_b_ref[...]
    row0 = CONV_PAD - (CONV_WIDTH - 1)
    span = ROW_PHASES * SUBLANES
    for rb in range(n_blocks):
        pieces[rb]()
        r0 = rb * CONV_ROWS
        sets = [(a, p) for a in range(CONV_ROWS // span) for p in range(ROW_PHASES)]

        def rows_of(a, p, shift=0):
            return pl.ds(shift + r0 + span * a + p, SUBLANES, stride=ROW_PHASES)

        acc = [[jnp.broadcast_to(conv_b_ref[:, LANES * c:LANES * (c + 1)], (SUBLANES, LANES))
                for c in range(CONV_SLABS)] for _ in sets]
        for k in range(CONV_WIDTH):
            for c in range(CONV_SLABS):
                w_kc = conv_w_ref[k:k + 1, LANES * c:LANES * (c + 1)]
                for i, (a, p) in enumerate(sets):
                    acc[i][c] = acc[i][c] + w_kc * u_hist[c, rows_of(a, p, row0 + k), :]
        conv = jnp.concatenate([jnp.concatenate(acc_i, axis=1) for acc_i in acc], axis=0)
        gate = jnp.concatenate(
            [jnp.concatenate([cg_s[c, rows_of(a, p), :] for c in range(CONV_SLABS)], axis=1)
             for a, p in sets], axis=0)
        mu = jnp.mean(conv, axis=-1, keepdims=True)
        cen = conv - mu
        var = jnp.mean(cen * cen, axis=-1, keepdims=True)
        y = cen * lax.rsqrt(var + LN_EPS) * cln_g + cln_b
        act = _silu(y) * gate
        for i, (a, p) in enumerate(sets):
            for c in range(CONV_SLABS):
                a_conv[c, rows_of(a, p), :] = act[SUBLANES * i:SUBLANES * (i + 1),
                                                  LANES * c:LANES * (c + 1)]

    first = s_idx == 0
    off_first = jnp.where(first, jnp.float32(-NEG), jnp.float32(0.0))
    lane_g = lax.broadcasted_iota(jnp.int32, (GROUP, LANES), 1)
    low_g = lane_g < HEAD_DIM
    col_tiles = KWIN // LANES
    band_tiles = (LEFT_CHUNKS + 1) * CHUNK // LANES + 1
    zero_tile = jnp.zeros((CHUNK, LANES), bf16)
    for g in range(TILE // GROUP):
        rows = slice(GROUP * g, GROUP * (g + 1))
        win = slice(GROUP * g, GROUP * g + KWIN)
        n_pre = (HIST - GROUP * g) // GROUP
        offs = [off_first if t < n_pre else None for t in range(KWIN // GROUP)]
        for j in range(N_HEADS // 2):
            qp = q_s[rows, LANES * j:LANES * (j + 1)]
            o_heads = []
            for hh in range(2):
                h = 2 * j + hh
                kh_t = kt_hist[LANES * h:LANES * (h + 1), win]
                s = jnp.dot(qp, kh_t, preferred_element_type=f32)
                e_rows = []
                for ci in range(GROUP // CHUNK):
                    r = slice(CHUNK * ci, CHUNK * (ci + 1))
                    first_tile = (CHUNK * ci) // LANES
                    live = range(first_tile, first_tile + band_tiles)
                    sc = {t: s[r, LANES * t:LANES * (t + 1)] + bias_ref[h, r, LANES * t:LANES * (t + 1)]
                          for t in live}
                    tops = [sc[t] if offs[LANES * t // GROUP] is None
                            else sc[t] - offs[LANES * t // GROUP] for t in live]
                    m = jnp.max(functools.reduce(jnp.maximum, tops), axis=-1, keepdims=True)
                    e_tiles = []
                    for t in range(col_tiles):
                        if t in live:
                            off = offs[LANES * t // GROUP]
                            mt = m if off is None else m + off
                            e_tiles.append(jnp.exp2(sc[t] - mt).astype(bf16))
                        else:
                            e_tiles.append(zero_tile)
                    e_rows.append(jnp.concatenate(e_tiles, axis=1))
                e = jnp.concatenate(e_rows, axis=0)
                vh = v_hist[win, LANES * h:LANES * (h + 1)]
                o_heads.append(jnp.dot(e, vh, preferred_element_type=f32))
            num = jnp.where(low_g, o_heads[0], o_heads[1])
            den = jnp.where(low_g, pltpu.roll(o_heads[0], HEAD_DIM, 1),
                            pltpu.roll(o_heads[1], HEAD_DIM, 1))
            o_s[rows, LANES * j:LANES * (j + 1)] = (
                num / den * ag_s[rows, LANES * j:LANES * (j + 1)]).astype(bf16)

    attn_out = jnp.dot(o_s[...], w_ao_ref[...], preferred_element_type=f32)
    a_all = jnp.concatenate([a_conv[c] for c in range(CONV_SLABS)], axis=1).astype(bf16)
    conv_out = jnp.dot(a_all, w_co_ref[...], preferred_element_type=f32)
    h = gc_s[...] * conv_out.astype(bf16) + ga_s[...] * attn_out.astype(bf16)
    y = jnp.dot(h, w_o_ref[...], preferred_element_type=f32) + b_o_ref[...]
    r = DEEPNORM_ALPHA * x_ref[...] + y
    mu = jnp.mean(r, axis=-1, keepdims=True)
    cen = r - mu
    var = jnp.mean(cen * cen, axis=-1, keepdims=True)
    out_ref[...] = cen * lax.rsqrt(var + LN_EPS) * oln_g_ref[...] + oln_b_ref[...]

    u_hist[:, 0:CONV_PAD, :] = u_hist[:, TILE:U_ROWS, :]
    kt_hist[:, 0:HIST] = kt_hist[:, TILE:TILE + HIST]
    v_hist[0:HIST, :] = v_hist[TILE:TILE + HIST, :]


def _rel_bias_table(rel_bias):
    n = GROUP + KWIN
    diag = np.arange(n)
    idx = np.clip(HIST + (GROUP - 1) - diag, -MAX_REL, MAX_REL) + MAX_REL
    n_far = int(np.sum(idx == 2 * MAX_REL)) - 1
    n_near = int(np.sum(idx == 0)) - 1
    rb = rel_bias.astype(jnp.float32)
    vec = jnp.concatenate([
        jnp.broadcast_to(rb[:, -1:], (N_HEADS, n_far)),
        rb[:, ::-1],
        jnp.broadcast_to(rb[:, :1], (N_HEADS, n_near))], axis=1)
    assert vec.shape == (N_HEADS, n)
    skew = jnp.tile(vec, (1, GROUP))[:, :GROUP * (n - 1)].reshape(N_HEADS, GROUP, n - 1)
    table = skew[:, :, GROUP - 1:GROUP - 1 + KWIN]
    r = np.arange(GROUP)[:, None]
    c = np.arange(KWIN)[None, :]
    band = c // CHUNK - r // CHUNK
    in_band = (band >= 0) & (band <= LEFT_CHUNKS)
    return jnp.where(jnp.asarray(in_band)[None], table * LOG2E, NEG)


def kernel(x, w_in, b_in, conv_w, conv_b, conv_ln_g, conv_ln_b, w_conv_out, rel_bias,
           w_attn_out, w_o, b_o, out_ln_g, out_ln_b):
    bsz, seq, d_model = x.shape
    assert d_model == D_MODEL and seq % TILE == 0 and w_in.shape == (D_MODEL, D_IN)
    f32, bf16 = jnp.float32, jnp.bfloat16

    def row(v):
        return v.astype(f32).reshape(1, -1)

    def full(shape):
        return pl.BlockSpec(shape, lambda b, s: (0,) * len(shape))

    operands = [
        (x, pl.BlockSpec((None, TILE, D_MODEL), lambda b, s: (b, s, 0))),
        (w_in.astype(bf16), full((D_MODEL, D_IN))),
        (row(b_in), full((1, D_IN))),
        (conv_w.astype(f32), full((CONV_WIDTH, D_CONV))),
        (row(conv_b), full((1, D_CONV))),
        (row(conv_ln_g), full((1, D_CONV))),
        (row(conv_ln_b), full((1, D_CONV))),
        (w_conv_out.astype(bf16), full((D_CONV, D_MODEL))),
        (_rel_bias_table(rel_bias), full((N_HEADS, GROUP, KWIN))),
        (w_attn_out.astype(bf16), full((D_ATTN, D_MODEL))),
        (w_o.astype(bf16), full((D_MODEL, D_MODEL))),
        (row(b_o), full((1, D_MODEL))),
        (row(out_ln_g), full((1, D_MODEL))),
        (row(out_ln_b), full((1, D_MODEL))),
    ]
    scratch = [
        pltpu.VMEM((TILE, D_MODEL), bf16),
        pltpu.VMEM((CONV_SLABS, U_ROWS, LANES), f32),
        pltpu.VMEM((CONV_SLABS, TILE, LANES), f32),
        pltpu.VMEM((CONV_SLABS, TILE, LANES), f32),
        pltpu.VMEM((TILE, D_MODEL), bf16),
        pltpu.VMEM((TILE, D_MODEL), bf16),
        pltpu.VMEM((TILE, D_ATTN), bf16),
        pltpu.VMEM((2 * D_ATTN, HIST + TILE), bf16),
        pltpu.VMEM((HIST + TILE, 2 * D_ATTN), bf16),
        pltpu.VMEM((TILE, D_ATTN), bf16),
        pltpu.VMEM((TILE, D_ATTN), bf16),
    ]
    return pl.pallas_call(
        _layer_kernel,
        grid=(bsz, seq // TILE),
        in_specs=[spec for _, spec in operands],
        out_specs=pl.BlockSpec((None, TILE, D_MODEL), lambda b, s: (b, s, 0)),
        out_shape=jax.ShapeDtypeStruct(x.shape, x.dtype),
        scratch_shapes=scratch,
        compiler_params=pltpu.CompilerParams(
            dimension_semantics=("arbitrary", "arbitrary"),
            vmem_limit_bytes=VMEM_LIMIT_BYTES),
        name="hybrid_layer",
    )(*[a for a, _ in operands])
```

```python
import functools
import math

import numpy as np
import jax
import jax.numpy as jnp
from jax import lax
from jax.experimental import pallas as pl
from jax.experimental.pallas import tpu as pltpu

D_MODEL = 1024
CHUNK = 64
LEFT_CHUNKS = 8
D_CONV = D_MODEL // 2
CONV_WIDTH = 31
N_HEADS = 8
HEAD_DIM = 64
D_ATTN = N_HEADS * HEAD_DIM
MAX_REL = 128
LN_EPS = 1e-5
DEPTH = 1
DEEPNORM_ALPHA = (2.0 * DEPTH) ** 0.25

C_VAL, C_GLU, C_GATE = 0, D_CONV, 2 * D_CONV
Q_OFF = 3 * D_CONV
K_OFF = Q_OFF + D_ATTN
V_OFF = K_OFF + D_ATTN
AG_OFF = V_OFF + D_ATTN
GC_OFF = AG_OFF + D_ATTN
GA_OFF = GC_OFF + D_MODEL
D_IN = GA_OFF + D_MODEL

LANES = 128
SUBLANES = 8
MXU_COLS = 256
TILE = 512
GROUP = 4 * CHUNK
HIST = LEFT_CHUNKS * CHUNK
KWIN = HIST + GROUP
CONV_PAD = 32
CONV_ROWS = 32
U_ROWS = CONV_PAD + TILE
CONV_SLABS = D_CONV // LANES
ROW_PHASES = 2
NEG = -1e30
LOG2E = math.log2(math.e)
VMEM_LIMIT_BYTES = 58 * 1024 * 1024

assert TILE == HIST and TILE % GROUP == 0 and CONV_PAD >= CONV_WIDTH - 1
assert CONV_PAD % SUBLANES == 0 and CONV_ROWS % (ROW_PHASES * SUBLANES) == 0


def _sigmoid(x):
    return jax.nn.sigmoid(x)


def _silu(x):
    return x * jax.nn.sigmoid(x)


def _layer_kernel(x_ref, w_in_ref, b_in_ref, conv_w_ref, conv_b_ref, cln_g_ref, cln_b_ref,
                  w_co_ref, bias_ref, w_ao_ref, w_o_ref, b_o_ref, oln_g_ref, oln_b_ref,
                  out_ref,
                  xb_s, u_hist, cg_s, a_conv, gc_s, ga_s, q_s, kt_hist, v_hist, ag_s, o_s):
    f32, bf16 = jnp.float32, jnp.bfloat16
    s_idx = pl.program_id(1)

    @pl.when(s_idx == 0)
    def _():
        u_hist[:, 0:CONV_PAD, :] = jnp.zeros((CONV_SLABS, CONV_PAD, LANES), f32)
        kt_hist[...] = jnp.zeros(kt_hist.shape, bf16)
        v_hist[0:HIST, :] = jnp.zeros((HIST, 2 * D_ATTN), bf16)

    xb_s[...] = x_ref[...].astype(bf16)

    def proj(lo, width):
        z = jnp.dot(xb_s[...], w_in_ref[:, lo:lo + width], preferred_element_type=f32)
        return z + b_in_ref[:, lo:lo + width]

    u = proj(C_VAL, D_CONV) * _sigmoid(proj(C_GLU, D_CONV))
    cg = _silu(proj(C_GATE, D_CONV))
    for c in range(CONV_SLABS):
        u_hist[c, CONV_PAD:U_ROWS, :] = u[:, LANES * c:LANES * (c + 1)]
        cg_s[c] = cg[:, LANES * c:LANES * (c + 1)]

    lane = lax.broadcasted_iota(jnp.int32, (TILE, LANES), 1)
    low = lane < HEAD_DIM

    def q_piece(c):
        q_s[:, c:c + MXU_COLS] = (
            proj(Q_OFF + c, MXU_COLS) * (LOG2E / math.sqrt(HEAD_DIM))).astype(bf16)

    def k_piece(c):
        kv = proj(K_OFF + c, MXU_COLS)
        for jj in range(MXU_COLS // LANES):
            pair_t = kv[:, LANES * jj:LANES * (jj + 1)].T.astype(bf16)
            h0 = 2 * (c // LANES + jj)
            kt_hist[LANES * h0:LANES * h0 + HEAD_DIM, HIST:HIST + TILE] = pair_t[0:HEAD_DIM]
            kt_hist[LANES * (h0 + 1) + HEAD_DIM:LANES * (h0 + 2), HIST:HIST + TILE] = (
                pair_t[HEAD_DIM:LANES])

    def v_piece(c):
        kv = proj(V_OFF + c, MXU_COLS)
        for jj in range(MXU_COLS // LANES):
            pair = kv[:, LANES * jj:LANES * (jj + 1)]
            base = 2 * (c + LANES * jj)
            v_hist[HIST:HIST + TILE, base:base + LANES] = jnp.where(low, pair, 1.0).astype(bf16)
            v_hist[HIST:HIST + TILE, base + LANES:base + 2 * LANES] = (
                jnp.where(low, 1.0, pair).astype(bf16))

    def gate_piece(off, dst, act, c):
        dst[:, c:c + MXU_COLS] = act(proj(off + c, MXU_COLS)).astype(bf16)

    pieces = []
    for c in range(0, D_ATTN, MXU_COLS):
        pieces.append(functools.partial(q_piece, c))
    for c in range(0, D_ATTN, MXU_COLS):
        pieces.append(functools.partial(k_piece, c))
    for c in range(0, D_ATTN, MXU_COLS):
        pieces.append(functools.partial(v_piece, c))
    for c in range(0, D_ATTN, MXU_COLS):
        pieces.append(functools.partial(gate_piece, AG_OFF, ag_s, _silu, c))
    for c in range(0, D_MODEL, MXU_COLS):
        pieces.append(functools.partial(gate_piece, GC_OFF, gc_s, _sigmoid, c))
    for c in range(0, D_MODEL, MXU_COLS):
        pieces.append(functools.partial(gate_piece, GA_OFF, ga_s, _sigmoid, c))
    n_blocks = TILE // CONV_ROWS
    assert len(pieces) == n_blocks

    cln_g = cln_g_ref[...]
    cln_b = cln_b_ref[...]
    row0 = CONV_PAD - (CONV_WIDTH - 1)
    span = ROW_PHASES * SUBLANES
    def conv_block(rb):
        r0 = rb * CONV_ROWS
        sets = [(a, p) for a in range(CONV_ROWS // span) for p in range(ROW_PHASES)]

        def rows_of(a, p, shift=0):
            return pl.ds(shift + r0 + span * a + p, SUBLANES, stride=ROW_PHASES)

        acc = [[jnp.broadcast_to(conv_b_ref[:, LANES * c:LANES * (c + 1)], (SUBLANES, LANES))
                for c in range(CONV_SLABS)] for _ in sets]
        loaded = {}

        def taps(c, a, p, k):
            key = (c, a, p + k)
            if key not in loaded:
                loaded[key] = u_hist[c, rows_of(a, p, row0 + k), :]
            return loaded[key]

        for k in range(CONV_WIDTH):
            for c in range(CONV_SLABS):
                w_kc = conv_w_ref[k:k + 1, LANES * c:LANES * (c + 1)]
                for i, (a, p) in enumerate(sets):
                    acc[i][c] = acc[i][c] + w_kc * taps(c, a, p, k)
        conv = jnp.concatenate([jnp.concatenate(acc_i, axis=1) for acc_i in acc], axis=0)
        gate = jnp.concatenate(
            [jnp.concatenate([cg_s[c, rows_of(a, p), :] for c in range(CONV_SLABS)], axis=1)
             for a, p in sets], axis=0)
        mu = jnp.mean(conv, axis=-1, keepdims=True)
        cen = conv - mu
        var = jnp.mean(cen * cen, axis=-1, keepdims=True)
        y = cen * lax.rsqrt(var + LN_EPS) * cln_g + cln_b
        act = _silu(y) * gate
        for i, (a, p) in enumerate(sets):
            for c in range(CONV_SLABS):
                a_conv[c, rows_of(a, p), :] = act[SUBLANES * i:SUBLANES * (i + 1),
                                                  LANES * c:LANES * (c + 1)]

    for rb in range(n_blocks):
        pieces[rb]()
        conv_block(rb)

    first = s_idx == 0
    off_first = jnp.where(first, jnp.float32(-NEG), jnp.float32(0.0))
    lane_g = lax.broadcasted_iota(jnp.int32, (GROUP, LANES), 1)
    low_g = lane_g < HEAD_DIM
    col_tiles = KWIN // LANES
    band_tiles = (LEFT_CHUNKS + 1) * CHUNK // LANES + 1
    zero_tile = jnp.zeros((CHUNK, LANES), bf16)
    for g in range(TILE // GROUP):
        rows = slice(GROUP * g, GROUP * (g + 1))
        win = slice(GROUP * g, GROUP * g + KWIN)
        n_pre = (HIST - GROUP * g) // GROUP
        offs = [off_first if t < n_pre else None for t in range(KWIN // GROUP)]
        for j in range(N_HEADS // 2):
            qp = q_s[rows, LANES * j:LANES * (j + 1)]
            o_heads = []
            for hh in range(2):
                h = 2 * j + hh
                kh_t = kt_hist[LANES * h:LANES * (h + 1), win]
                s = jnp.dot(qp, kh_t, preferred_element_type=f32)
                e_rows = []
                for ci in range(GROUP // CHUNK):
                    r = slice(CHUNK * ci, CHUNK * (ci + 1))
                    first_tile = (CHUNK * ci) // LANES
                    live = range(first_tile, first_tile + band_tiles)
                    sc = {t: s[r, LANES * t:LANES * (t + 1)] + bias_ref[h, r, LANES * t:LANES * (t + 1)]
                          for t in live}
                    tops = [sc[t] if offs[LANES * t // GROUP] is None
                            else sc[t] - offs[LANES * t // GROUP] for t in live]
                    m = jnp.max(functools.reduce(jnp.maximum, tops), axis=-1, keepdims=True)
                    e_tiles = []
                    for t in range(col_tiles):
                        if t in live:
                            off = offs[LANES * t // GROUP]
                            mt = m if off is None else m + off
                            e_tiles.append(jnp.exp2(sc[t] - mt).astype(bf16))
                        else:
                            e_tiles.append(zero_tile)
                    e_rows.append(jnp.concatenate(e_tiles, axis=1))
                e = jnp.concatenate(e_rows, axis=0)
                vh = v_hist[win, LANES * h:LANES * (h + 1)]
                o_heads.append(jnp.dot(e, vh, preferred_element_type=f32))
            num = jnp.where(low_g, o_heads[0], o_heads[1])
            den = jnp.where(low_g, pltpu.roll(o_heads[0], HEAD_DIM, 1),
                            pltpu.roll(o_heads[1], HEAD_DIM, 1))
            o_s[rows, LANES * j:LANES * (j + 1)] = (
                num / den * ag_s[rows, LANES * j:LANES * (j + 1)]).astype(bf16)

    attn_out = jnp.dot(o_s[...], w_ao_ref[...], preferred_element_type=f32)
    a_all = jnp.concatenate([a_conv[c] for c in range(CONV_SLABS)], axis=1).astype(bf16)
    conv_out = jnp.dot(a_all, w_co_ref[...], preferred_element_type=f32)
    h = gc_s[...] * conv_out.astype(bf16) + ga_s[...] * attn_out.astype(bf16)
    y = jnp.dot(h, w_o_ref[...], preferred_element_type=f32) + b_o_ref[...]
    r = DEEPNORM_ALPHA * x_ref[...] + y
    mu = jnp.mean(r, axis=-1, keepdims=True)
    cen = r - mu
    var = jnp.mean(cen * cen, axis=-1, keepdims=True)
    out_ref[...] = cen * lax.rsqrt(var + LN_EPS) * oln_g_ref[...] + oln_b_ref[...]

    u_hist[:, 0:CONV_PAD, :] = u_hist[:, TILE:U_ROWS, :]
    kt_hist[:, 0:HIST] = kt_hist[:, TILE:TILE + HIST]
    v_hist[0:HIST, :] = v_hist[TILE:TILE + HIST, :]


def _rel_bias_table(rel_bias):
    n = GROUP + KWIN
    diag = np.arange(n)
    idx = np.clip(HIST + (GROUP - 1) - diag, -MAX_REL, MAX_REL) + MAX_REL
    n_far = int(np.sum(idx == 2 * MAX_REL)) - 1
    n_near = int(np.sum(idx == 0)) - 1
    rb = rel_bias.astype(jnp.float32)
    vec = jnp.concatenate([
        jnp.broadcast_to(rb[:, -1:], (N_HEADS, n_far)),
        rb[:, ::-1],
        jnp.broadcast_to(rb[:, :1], (N_HEADS, n_near))], axis=1)
    assert vec.shape == (N_HEADS, n)
    skew = jnp.tile(vec, (1, GROUP))[:, :GROUP * (n - 1)].reshape(N_HEADS, GROUP, n - 1)
    table = skew[:, :, GROUP - 1:GROUP - 1 + KWIN]
    r = np.arange(GROUP)[:, None]
    c = np.arange(KWIN)[None, :]
    band = c // CHUNK - r // CHUNK
    in_band = (band >= 0) & (band <= LEFT_CHUNKS)
    return jnp.where(jnp.asarray(in_band)[None], table * LOG2E, NEG)


def kernel(x, w_in, b_in, conv_w, conv_b, conv_ln_g, conv_ln_b, w_conv_out, rel_bias,
           w_attn_out, w_o, b_o, out_ln_g, out_ln_b):
    bsz, seq, d_model = x.shape
    assert d_model == D_MODEL and seq % TILE == 0 and w_in.shape == (D_MODEL, D_IN)
    f32, bf16 = jnp.float32, jnp.bfloat16

    def row(v):
        return v.astype(f32).reshape(1, -1)

    def full(shape):
        return pl.BlockSpec(shape, lambda b, s: (0,) * len(shape))

    operands = [
        (x, pl.BlockSpec((None, TILE, D_MODEL), lambda b, s: (b, s, 0))),
        (w_in.astype(bf16), full((D_MODEL, D_IN))),
        (row(b_in), full((1, D_IN))),
        (conv_w.astype(f32), full((CONV_WIDTH, D_CONV))),
        (row(conv_b), full((1, D_CONV))),
        (row(conv_ln_g), full((1, D_CONV))),
        (row(conv_ln_b), full((1, D_CONV))),
        (w_conv_out.astype(bf16), full((D_CONV, D_MODEL))),
        (_rel_bias_table(rel_bias), full((N_HEADS, GROUP, KWIN))),
        (w_attn_out.astype(bf16), full((D_ATTN, D_MODEL))),
        (w_o.astype(bf16), full((D_MODEL, D_MODEL))),
        (row(b_o), full((1, D_MODEL))),
        (row(out_ln_g), full((1, D_MODEL))),
        (row(out_ln_b), full((1, D_MODEL))),
    ]
    scratch = [
        pltpu.VMEM((TILE, D_MODEL), bf16),
        pltpu.VMEM((CONV_SLABS, U_ROWS, LANES), f32),
        pltpu.VMEM((CONV_SLABS, TILE, LANES), f32),
        pltpu.VMEM((CONV_SLABS, TILE, LANES), f32),
        pltpu.VMEM((TILE, D_MODEL), bf16),
        pltpu.VMEM((TILE, D_MODEL), bf16),
        pltpu.VMEM((TILE, D_ATTN), bf16),
        pltpu.VMEM((2 * D_ATTN, HIST + TILE), bf16),
        pltpu.VMEM((HIST + TILE, 2 * D_ATTN), bf16),
        pltpu.VMEM((TILE, D_ATTN), bf16),
        pltpu.VMEM((TILE, D_ATTN), bf16),
    ]
    return pl.pallas_call(
        _layer_kernel,
        grid=(bsz, seq // TILE),
        in_specs=[spec for _, spec in operands],
        out_specs=pl.BlockSpec((None, TILE, D_MODEL), lambda b, s: (b, s, 0)),
        out_shape=jax.ShapeDtypeStruct(x.shape, x.dtype),
        scratch_shapes=scratch,
        compiler_params=pltpu.CompilerParams(
            dimension_semantics=("arbitrary", "arbitrary"),
            vmem_limit_bytes=VMEM_LIMIT_BYTES),
        name="hybrid_layer",
    )(*[a for a, _ in operands])
```

```python
import functools
import math

import numpy as np
import jax
import jax.numpy as jnp
from jax import lax
from jax.experimental import pallas as pl
from jax.experimental.pallas import tpu as pltpu

D_MODEL = 1024
CHUNK = 64
LEFT_CHUNKS = 8
D_CONV = D_MODEL // 2
CONV_WIDTH = 31
N_HEADS = 8
HEAD_DIM = 64
D_ATTN = N_HEADS * HEAD_DIM
MAX_REL = 128
LN_EPS = 1e-5
DEPTH = 1
DEEPNORM_ALPHA = (2.0 * DEPTH) ** 0.25

C_VAL, C_GLU, C_GATE = 0, D_CONV, 2 * D_CONV
Q_OFF = 3 * D_CONV
K_OFF = Q_OFF + D_ATTN
V_OFF = K_OFF + D_ATTN
AG_OFF = V_OFF + D_ATTN
GC_OFF = AG_OFF + D_ATTN
GA_OFF = GC_OFF + D_MODEL
D_IN = GA_OFF + D_MODEL

LANES = 128
SUBLANES = 8
MXU_COLS = 256
TILE = 512
GROUP = 4 * CHUNK
HIST = LEFT_CHUNKS * CHUNK
KWIN = HIST + GROUP
CONV_PAD = 32
CONV_ROWS = 32
U_ROWS = CONV_PAD + TILE
CONV_SLABS = D_CONV // LANES
ROW_PHASES = 2
NEG = -1e30
LOG2E = math.log2(math.e)
VMEM_LIMIT_BYTES = 58 * 1024 * 1024

assert TILE == HIST and TILE % GROUP == 0 and CONV_PAD >= CONV_WIDTH - 1
assert CONV_PAD % SUBLANES == 0 and CONV_ROWS % (ROW_PHASES * SUBLANES) == 0


def _sigmoid(x):
    return jax.nn.sigmoid(x)


def _silu(x):
    return x * jax.nn.sigmoid(x)


def _layer_kernel(x_ref, w_in_ref, b_in_ref, conv_w_ref, conv_b_ref, cln_g_ref, cln_b_ref,
                  w_co_ref, bias_ref, w_ao_ref, w_o_ref, b_o_ref, oln_g_ref, oln_b_ref,
                  out_ref,
                  xb_s, u_hist, cg_s, a_conv, gc_s, ga_s, q_s, kt_hist, v_hist, ag_s, o_s):
    f32, bf16 = jnp.float32, jnp.bfloat16
    s_idx = pl.program_id(1)

    @pl.when(s_idx == 0)
    def _():
        u_hist[:, 0:CONV_PAD, :] = jnp.zeros((CONV_SLABS, CONV_PAD, LANES), f32)
        kt_hist[...] = jnp.zeros(kt_hist.shape, bf16)
        v_hist[0:HIST, :] = jnp.zeros((HIST, 2 * D_ATTN), bf16)

    xb_s[...] = x_ref[...].astype(bf16)

    def proj(lo, width):
        z = jnp.dot(xb_s[...], w_in_ref[:, lo:lo + width], preferred_element_type=f32)
        return z + b_in_ref[:, lo:lo + width]

    u = proj(C_VAL, D_CONV) * _sigmoid(proj(C_GLU, D_CONV))
    cg = _silu(proj(C_GATE, D_CONV))
    for c in range(CONV_SLABS):
        u_hist[c, CONV_PAD:U_ROWS, :] = u[:, LANES * c:LANES * (c + 1)]
        cg_s[c] = cg[:, LANES * c:LANES * (c + 1)]

    lane = lax.broadcasted_iota(jnp.int32, (TILE, LANES), 1)
    low = lane < HEAD_DIM

    def q_piece(c):
        q_s[:, c:c + MXU_COLS] = (
            proj(Q_OFF + c, MXU_COLS) * (LOG2E / math.sqrt(HEAD_DIM))).astype(bf16)

    def k_piece(c):
        kv = proj(K_OFF + c, MXU_COLS)
        for jj in range(MXU_COLS // LANES):
            pair_t = kv[:, LANES * jj:LANES * (jj + 1)].T.astype(bf16)
            h0 = 2 * (c // LANES + jj)
            kt_hist[LANES * h0:LANES * h0 + HEAD_DIM, HIST:HIST + TILE] = pair_t[0:HEAD_DIM]
            kt_hist[LANES * (h0 + 1) + HEAD_DIM:LANES * (h0 + 2), HIST:HIST + TILE] = (
                pair_t[HEAD_DIM:LANES])

    def v_piece(c):
        kv = proj(V_OFF + c, MXU_COLS)
        for jj in range(MXU_COLS // LANES):
            pair = kv[:, LANES * jj:LANES * (jj + 1)]
            base = 2 * (c + LANES * jj)
            v_hist[HIST:HIST + TILE, base:base + LANES] = jnp.where(low, pair, 1.0).astype(bf16)
            v_hist[HIST:HIST + TILE, base + LANES:base + 2 * LANES] = (
                jnp.where(low, 1.0, pair).astype(bf16))

    def gate_piece(off, dst, act, c):
        dst[:, c:c + MXU_COLS] = act(proj(off + c, MXU_COLS)).astype(bf16)

    pieces = []
    for c in range(0, D_ATTN, MXU_COLS):
        pieces.append(functools.partial(q_piece, c))
    for c in range(0, D_ATTN, MXU_COLS):
        pieces.append(functools.partial(k_piece, c))
    for c in range(0, D_ATTN, MXU_COLS):
        pieces.append(functools.partial(v_piece, c))
    for c in range(0, D_ATTN, MXU_COLS):
        pieces.append(functools.partial(gate_piece, AG_OFF, ag_s, _silu, c))
    for c in range(0, D_MODEL, MXU_COLS):
        pieces.append(functools.partial(gate_piece, GC_OFF, gc_s, _sigmoid, c))
    for c in range(0, D_MODEL, MXU_COLS):
        pieces.append(functools.partial(gate_piece, GA_OFF, ga_s, _sigmoid, c))
    n_blocks = TILE // CONV_ROWS
    assert len(pieces) == n_blocks

    cln_g = cln_g_ref[...]
    cln_b = cln_b_ref[...]
    row0 = CONV_PAD - (CONV_WIDTH - 1)
    span = ROW_PHASES * SUBLANES
    def conv_block(rb):
        r0 = rb * CONV_ROWS
        sets = [(a, p) for a in range(CONV_ROWS // span) for p in range(ROW_PHASES)]

        def rows_of(a, p, shift=0):
            return pl.ds(shift + r0 + span * a + p, SUBLANES, stride=ROW_PHASES)

        acc = [[jnp.broadcast_to(conv_b_ref[:, LANES * c:LANES * (c + 1)], (SUBLANES, LANES))
                for c in range(CONV_SLABS)] for _ in sets]
        loaded = {}

        def taps(c, a, p, k):
            key = (c, a, p + k)
            if key not in loaded:
                loaded[key] = u_hist[c, rows_of(a, p, row0 + k), :]
            return loaded[key]

        for k in range(CONV_WIDTH):
            for c in range(CONV_SLABS):
                w_kc = conv_w_ref[k:k + 1, LANES * c:LANES * (c + 1)]
                for i, (a, p) in enumerate(sets):
                    acc[i][c] = acc[i][c] + w_kc * taps(c, a, p, k)
        conv = jnp.concatenate([jnp.concatenate(acc_i, axis=1) for acc_i in acc], axis=0)
        gate = jnp.concatenate(
            [jnp.concatenate([cg_s[c, rows_of(a, p), :] for c in range(CONV_SLABS)], axis=1)
             for a, p in sets], axis=0)
        mu = jnp.mean(conv, axis=-1, keepdims=True)
        cen = conv - mu
        var = jnp.mean(cen * cen, axis=-1, keepdims=True)
        y = cen * lax.rsqrt(var + LN_EPS) * cln_g + cln_b
        act = _silu(y) * gate
        for i, (a, p) in enumerate(sets):
            for c in range(CONV_SLABS):
                a_conv[c, rows_of(a, p), :] = act[SUBLANES * i:SUBLANES * (i + 1),
                                                  LANES * c:LANES * (c + 1)]

    for rb in range(n_blocks):
        pieces[rb]()
        conv_block(rb)

    first = s_idx == 0
    off_first = jnp.where(first, jnp.float32(-NEG), jnp.float32(0.0))
    lane_g = lax.broadcasted_iota(jnp.int32, (GROUP, LANES), 1)
    low_g = lane_g < HEAD_DIM
    col_tiles = KWIN // LANES
    band_tiles = (LEFT_CHUNKS + 1) * CHUNK // LANES + 1
    zero_tile = jnp.zeros((CHUNK, LANES), bf16)
    for g in range(TILE // GROUP):
        rows = slice(GROUP * g, GROUP * (g + 1))
        win = slice(GROUP * g, GROUP * g + KWIN)
        n_pre = (HIST - GROUP * g) // GROUP
        offs = [off_first if t < n_pre else None for t in range(KWIN // GROUP)]
        for j in range(N_HEADS // 2):
            qp = q_s[rows, LANES * j:LANES * (j + 1)]
            o_heads = []
            for hh in range(2):
                h = 2 * j + hh
                kh_t = kt_hist[LANES * h:LANES * (h + 1), win]
                s = jnp.dot(qp, kh_t, preferred_element_type=f32)
                e_rows = []
                for ci in range(GROUP // CHUNK):
                    r = slice(CHUNK * ci, CHUNK * (ci + 1))
                    first_tile = (CHUNK * ci) // LANES
                    live = range(first_tile, first_tile + band_tiles)
                    sc = {t: s[r, LANES * t:LANES * (t + 1)] + bias_ref[h, r, LANES * t:LANES * (t + 1)]
                          for t in live}
                    tops = [sc[t] if offs[LANES * t // GROUP] is None
                            else sc[t] - offs[LANES * t // GROUP] for t in live]
                    m = jnp.max(functools.reduce(jnp.maximum, tops), axis=-1, keepdims=True)
                    e_tiles = []
                    for t in range(col_tiles):
                        if t in live:
                            off = offs[LANES * t // GROUP]
                            mt = m if off is None else m + off
                            e_tiles.append(jnp.exp2(sc[t] - mt).astype(bf16))
                        else:
                            e_tiles.append(zero_tile)
                    e_rows.append(jnp.concatenate(e_tiles, axis=1))
                e = jnp.concatenate(e_rows, axis=0)
                vh = v_hist[win, LANES * h:LANES * (h + 1)]
                o_heads.append(jnp.dot(e, vh, preferred_element_type=f32))
            num = jnp.where(low_g, o_heads[0], o_heads[1])
            den = jnp.where(low_g, pltpu.roll(o_heads[0], HEAD_DIM, 1),
                            pltpu.roll(o_heads[1], HEAD_DIM, 1))
            o_s[rows, LANES * j:LANES * (j + 1)] = (
                num / den * ag_s[rows, LANES * j:LANES * (j + 1)]).astype(bf16)

        attn_out = jnp.dot(o_s[rows, :], w_ao_ref[...], preferred_element_type=f32)
        a_rows = jnp.concatenate([a_conv[c, rows, :] for c in range(CONV_SLABS)], axis=1)
        conv_out = jnp.dot(a_rows.astype(bf16), w_co_ref[...], preferred_element_type=f32)
        h = gc_s[rows, :] * conv_out.astype(bf16) + ga_s[rows, :] * attn_out.astype(bf16)
        y = jnp.dot(h, w_o_ref[...], preferred_element_type=f32) + b_o_ref[...]
        r = DEEPNORM_ALPHA * x_ref[rows, :] + y
        mu = jnp.mean(r, axis=-1, keepdims=True)
        cen = r - mu
        var = jnp.mean(cen * cen, axis=-1, keepdims=True)
        out_ref[rows, :] = cen * lax.rsqrt(var + LN_EPS) * oln_g_ref[...] + oln_b_ref[...]

    u_hist[:, 0:CONV_PAD, :] = u_hist[:, TILE:U_ROWS, :]
    kt_hist[:, 0:HIST] = kt_hist[:, TILE:TILE + HIST]
    v_hist[0:HIST, :] = v_hist[TILE:TILE + HIST, :]


def _rel_bias_table(rel_bias):
    n = GROUP + KWIN
    diag = np.arange(n)
    idx = np.clip(HIST + (GROUP - 1) - diag, -MAX_REL, MAX_REL) + MAX_REL
    n_far = int(np.sum(idx == 2 * MAX_REL)) - 1
    n_near = int(np.sum(idx == 0)) - 1
    rb = rel_bias.astype(jnp.float32)
    vec = jnp.concatenate([
        jnp.broadcast_to(rb[:, -1:], (N_HEADS, n_far)),
        rb[:, ::-1],
        jnp.broadcast_to(rb[:, :1], (N_HEADS, n_near))], axis=1)
    assert vec.shape == (N_HEADS, n)
    skew = jnp.tile(vec, (1, GROUP))[:, :GROUP * (n - 1)].reshape(N_HEADS, GROUP, n - 1)
    table = skew[:, :, GROUP - 1:GROUP - 1 + KWIN]
    r = np.arange(GROUP)[:, None]
    c = np.arange(KWIN)[None, :]
    band = c // CHUNK - r // CHUNK
    in_band = (band >= 0) & (band <= LEFT_CHUNKS)
    return jnp.where(jnp.asarray(in_band)[None], table * LOG2E, NEG)


def kernel(x, w_in, b_in, conv_w, conv_b, conv_ln_g, conv_ln_b, w_conv_out, rel_bias,
           w_attn_out, w_o, b_o, out_ln_g, out_ln_b):
    bsz, seq, d_model = x.shape
    assert d_model == D_MODEL and seq % TILE == 0 and w_in.shape == (D_MODEL, D_IN)
    f32, bf16 = jnp.float32, jnp.bfloat16

    def row(v):
        return v.astype(f32).reshape(1, -1)

    def full(shape):
        return pl.BlockSpec(shape, lambda b, s: (0,) * len(shape))

    operands = [
        (x, pl.BlockSpec((None, TILE, D_MODEL), lambda b, s: (b, s, 0))),
        (w_in.astype(bf16), full((D_MODEL, D_IN))),
        (row(b_in), full((1, D_IN))),
        (conv_w.astype(f32), full((CONV_WIDTH, D_CONV))),
        (row(conv_b), full((1, D_CONV))),
        (row(conv_ln_g), full((1, D_CONV))),
        (row(conv_ln_b), full((1, D_CONV))),
        (w_conv_out.astype(bf16), full((D_CONV, D_MODEL))),
        (_rel_bias_table(rel_bias), full((N_HEADS, GROUP, KWIN))),
        (w_attn_out.astype(bf16), full((D_ATTN, D_MODEL))),
        (w_o.astype(bf16), full((D_MODEL, D_MODEL))),
        (row(b_o), full((1, D_MODEL))),
        (row(out_ln_g), full((1, D_MODEL))),
        (row(out_ln_b), full((1, D_MODEL))),
    ]
    scratch = [
        pltpu.VMEM((TILE, D_MODEL), bf16),
        pltpu.VMEM((CONV_SLABS, U_ROWS, LANES), f32),
        pltpu.VMEM((CONV_SLABS, TILE, LANES), f32),
        pltpu.VMEM((CONV_SLABS, TILE, LANES), f32),
        pltpu.VMEM((TILE, D_MODEL), bf16),
        pltpu.VMEM((TILE, D_MODEL), bf16),
        pltpu.VMEM((TILE, D_ATTN), bf16),
        pltpu.VMEM((2 * D_ATTN, HIST + TILE), bf16),
        pltpu.VMEM((HIST + TILE, 2 * D_ATTN), bf16),
        pltpu.VMEM((TILE, D_ATTN), bf16),
        pltpu.VMEM((TILE, D_ATTN), bf16),
    ]
    return pl.pallas_call(
        _layer_kernel,
        grid=(bsz, seq // TILE),
        in_specs=[spec for _, spec in operands],
        out_specs=pl.BlockSpec((None, TILE, D_MODEL), lambda b, s: (b, s, 0)),
        out_shape=jax.ShapeDtypeStruct(x.shape, x.dtype),
        scratch_shapes=scratch,
        compiler_params=pltpu.CompilerParams(
            dimension_semantics=("arbitrary", "arbitrary"),
            vmem_limit_bytes=VMEM_LIMIT_BYTES),
        name="hybrid_layer",
    )(*[a for a, _ in operands])
```

```python
import functools
import math

import numpy as np
import jax
import jax.numpy as jnp
from jax import lax
from jax.experimental import pallas as pl
from jax.experimental.pallas import tpu as pltpu

D_MODEL = 1024
CHUNK = 64
LEFT_CHUNKS = 8
D_CONV = D_MODEL // 2
CONV_WIDTH = 31
N_HEADS = 8
HEAD_DIM = 64
D_ATTN = N_HEADS * HEAD_DIM
MAX_REL = 128
LN_EPS = 1e-5
DEPTH = 1
DEEPNORM_ALPHA = (2.0 * DEPTH) ** 0.25

C_VAL, C_GLU, C_GATE = 0, D_CONV, 2 * D_CONV
Q_OFF = 3 * D_CONV
K_OFF = Q_OFF + D_ATTN
V_OFF = K_OFF + D_ATTN
AG_OFF = V_OFF + D_ATTN
GC_OFF = AG_OFF + D_ATTN
GA_OFF = GC_OFF + D_MODEL
D_IN = GA_OFF + D_MODEL

LANES = 128
SUBLANES = 8
MXU_COLS = 256
PIECE_COLS = 2 * MXU_COLS
TILE = 512
GROUP = 4 * CHUNK
HIST = LEFT_CHUNKS * CHUNK
KWIN = HIST + GROUP
CONV_PAD = 32
CONV_ROWS = 32
U_ROWS = CONV_PAD + TILE
CONV_SLABS = D_CONV // LANES
ROW_PHASES = 2
NEG = -1e30
LOG2E = math.log2(math.e)
VMEM_LIMIT_BYTES = 58 * 1024 * 1024

assert TILE == HIST and TILE % GROUP == 0 and CONV_PAD >= CONV_WIDTH - 1
assert CONV_PAD % SUBLANES == 0 and CONV_ROWS % (ROW_PHASES * SUBLANES) == 0


def _sigmoid(x):
    return jax.nn.sigmoid(x)


def _silu(x):
    return x * jax.nn.sigmoid(x)


def _layer_kernel(x_ref, w_in_ref, b_in_ref, conv_w_ref, conv_b_ref, cln_g_ref, cln_b_ref,
                  w_co_ref, bias_ref, w_ao_ref, w_o_ref, b_o_ref, oln_g_ref, oln_b_ref,
                  out_ref,
                  xb_s, u_hist, cg_s, a_conv, gc_s, ga_s, q_s, kt_hist, v_hist, ag_s, o_s):
    f32, bf16 = jnp.float32, jnp.bfloat16
    s_idx = pl.program_id(1)

    @pl.when(s_idx == 0)
    def _():
        u_hist[:, 0:CONV_PAD, :] = jnp.zeros((CONV_SLABS, CONV_PAD, LANES), f32)
        kt_hist[...] = jnp.zeros(kt_hist.shape, bf16)
        v_hist[0:HIST, :] = jnp.zeros((HIST, 2 * D_ATTN), bf16)

    xb_s[...] = x_ref[...].astype(bf16)

    def proj(lo, width):
        z = jnp.dot(xb_s[...], w_in_ref[:, lo:lo + width], preferred_element_type=f32)
        return z + b_in_ref[:, lo:lo + width]

    u = proj(C_VAL, D_CONV) * _sigmoid(proj(C_GLU, D_CONV))
    cg = _silu(proj(C_GATE, D_CONV))
    for c in range(CONV_SLABS):
        u_hist[c, CONV_PAD:U_ROWS, :] = u[:, LANES * c:LANES * (c + 1)]
        cg_s[c] = cg[:, LANES * c:LANES * (c + 1)]

    lane = lax.broadcasted_iota(jnp.int32, (TILE, LANES), 1)
    low = lane < HEAD_DIM

    def q_piece(c):
        q_s[:, c:c + PIECE_COLS] = (
            proj(Q_OFF + c, PIECE_COLS) * (LOG2E / math.sqrt(HEAD_DIM))).astype(bf16)

    def k_piece(c):
        kv = proj(K_OFF + c, PIECE_COLS)
        for jj in range(PIECE_COLS // LANES):
            pair_t = kv[:, LANES * jj:LANES * (jj + 1)].T.astype(bf16)
            h0 = 2 * (c // LANES + jj)
            kt_hist[LANES * h0:LANES * h0 + HEAD_DIM, HIST:HIST + TILE] = pair_t[0:HEAD_DIM]
            kt_hist[LANES * (h0 + 1) + HEAD_DIM:LANES * (h0 + 2), HIST:HIST + TILE] = (
                pair_t[HEAD_DIM:LANES])

    def v_piece(c):
        kv = proj(V_OFF + c, PIECE_COLS)
        for jj in range(PIECE_COLS // LANES):
            pair = kv[:, LANES * jj:LANES * (jj + 1)]
            base = 2 * (c + LANES * jj)
            v_hist[HIST:HIST + TILE, base:base + LANES] = jnp.where(low, pair, 1.0).astype(bf16)
            v_hist[HIST:HIST + TILE, base + LANES:base + 2 * LANES] = (
                jnp.where(low, 1.0, pair).astype(bf16))

    def gate_piece(off, dst, act, c):
        dst[:, c:c + PIECE_COLS] = act(proj(off + c, PIECE_COLS)).astype(bf16)

    pieces = []
    for c in range(0, D_ATTN, PIECE_COLS):
        pieces.append(functools.partial(q_piece, c))
    for c in range(0, D_ATTN, PIECE_COLS):
        pieces.append(functools.partial(k_piece, c))
    for c in range(0, D_ATTN, PIECE_COLS):
        pieces.append(functools.partial(v_piece, c))
    for c in range(0, D_ATTN, PIECE_COLS):
        pieces.append(functools.partial(gate_piece, AG_OFF, ag_s, _silu, c))
    for c in range(0, D_MODEL, PIECE_COLS):
        pieces.append(functools.partial(gate_piece, GC_OFF, gc_s, _sigmoid, c))
    for c in range(0, D_MODEL, PIECE_COLS):
        pieces.append(functools.partial(gate_piece, GA_OFF, ga_s, _sigmoid, c))
    n_blocks = TILE // CONV_ROWS
    blocks_per_piece = n_blocks // len(pieces)
    assert blocks_per_piece * len(pieces) == n_blocks

    cln_g = cln_g_ref[...]
    cln_b = cln_b_ref[...]
    row0 = CONV_PAD - (CONV_WIDTH - 1)
    span = ROW_PHASES * SUBLANES
    def conv_block(rb):
        r0 = rb * CONV_ROWS
        sets = [(a, p) for a in range(CONV_ROWS // span) for p in range(ROW_PHASES)]

        def rows_of(a, p, shift=0):
            return pl.ds(shift + r0 + span * a + p, SUBLANES, stride=ROW_PHASES)

        acc = [[jnp.broadcast_to(conv_b_ref[:, LANES * c:LANES * (c + 1)], (SUBLANES, LANES))
                for c in range(CONV_SLABS)] for _ in sets]
        loaded = {}

        def taps(c, a, p, k):
            key = (c, a, p + k)
            if key not in loaded:
                loaded[key] = u_hist[c, rows_of(a, p, row0 + k), :]
            return loaded[key]

        for k in range(CONV_WIDTH):
            for c in range(CONV_SLABS):
                w_kc = conv_w_ref[k:k + 1, LANES * c:LANES * (c + 1)]
                for i, (a, p) in enumerate(sets):
                    acc[i][c] = acc[i][c] + w_kc * taps(c, a, p, k)
        conv = jnp.concatenate([jnp.concatenate(acc_i, axis=1) for acc_i in acc], axis=0)
        gate = jnp.concatenate(
            [jnp.concatenate([cg_s[c, rows_of(a, p), :] for c in range(CONV_SLABS)], axis=1)
             for a, p in sets], axis=0)
        mu = jnp.mean(conv, axis=-1, keepdims=True)
        cen = conv - mu
        var = jnp.mean(cen * cen, axis=-1, keepdims=True)
        y = cen * lax.rsqrt(var + LN_EPS) * cln_g + cln_b
        act = _silu(y) * gate
        for i, (a, p) in enumerate(sets):
            for c in range(CONV_SLABS):
                a_conv[c, rows_of(a, p), :] = act[SUBLANES * i:SUBLANES * (i + 1),
                                                  LANES * c:LANES * (c + 1)]

    for n, piece in enumerate(pieces):
        piece()
        for rb in range(blocks_per_piece * n, blocks_per_piece * (n + 1)):
            conv_block(rb)

    first = s_idx == 0
    off_first = jnp.where(first, jnp.float32(-NEG), jnp.float32(0.0))
    lane_g = lax.broadcasted_iota(jnp.int32, (GROUP, LANES), 1)
    low_g = lane_g < HEAD_DIM
    col_tiles = KWIN // LANES
    band_tiles = (LEFT_CHUNKS + 1) * CHUNK // LANES + 1
    zero_tile = jnp.zeros((CHUNK, LANES), bf16)
    for g in range(TILE // GROUP):
        rows = slice(GROUP * g, GROUP * (g + 1))
        win = slice(GROUP * g, GROUP * g + KWIN)
        n_pre = (HIST - GROUP * g) // GROUP
        offs = [off_first if t < n_pre else None for t in range(KWIN // GROUP)]
        for j in range(N_HEADS // 2):
            qp = q_s[rows, LANES * j:LANES * (j + 1)]
            o_heads = []
            for hh in range(2):
                h = 2 * j + hh
                kh_t = kt_hist[LANES * h:LANES * (h + 1), win]
                s = jnp.dot(qp, kh_t, preferred_element_type=f32)
                e_rows = []
                for ci in range(GROUP // CHUNK):
                    r = slice(CHUNK * ci, CHUNK * (ci + 1))
                    first_tile = (CHUNK * ci) // LANES
                    live = range(first_tile, first_tile + band_tiles)
                    sc = {t: s[r, LANES * t:LANES * (t + 1)] + bias_ref[h, r, LANES * t:LANES * (t + 1)]
                          for t in live}
                    tops = [sc[t] if offs[LANES * t // GROUP] is None
                            else sc[t] - offs[LANES * t // GROUP] for t in live]
                    m = jnp.max(functools.reduce(jnp.maximum, tops), axis=-1, keepdims=True)
                    e_tiles = []
                    for t in range(col_tiles):
                        if t in live:
                            off = offs[LANES * t // GROUP]
                            mt = m if off is None else m + off
                            e_tiles.append(jnp.exp2(sc[t] - mt).astype(bf16))
                        else:
                            e_tiles.append(zero_tile)
                    e_rows.append(jnp.concatenate(e_tiles, axis=1))
                e = jnp.concatenate(e_rows, axis=0)
                vh = v_hist[win, LANES * h:LANES * (h + 1)]
                o_heads.append(jnp.dot(e, vh, preferred_element_type=f32))
            num = jnp.where(low_g, o_heads[0], o_heads[1])
            den = jnp.where(low_g, pltpu.roll(o_heads[0], HEAD_DIM, 1),
                            pltpu.roll(o_heads[1], HEAD_DIM, 1))
            o_s[rows, LANES * j:LANES * (j + 1)] = (
                num / den * ag_s[rows, LANES * j:LANES * (j + 1)]).astype(bf16)

    attn_out = jnp.dot(o_s[...], w_ao_ref[...], preferred_element_type=f32)
    a_all = jnp.concatenate([a_conv[c] for c in range(CONV_SLABS)], axis=1).astype(bf16)
    conv_out = jnp.dot(a_all, w_co_ref[...], preferred_element_type=f32)
    h = gc_s[...] * conv_out.astype(bf16) + ga_s[...] * attn_out.astype(bf16)
    y = jnp.dot(h, w_o_ref[...], preferred_element_type=f32) + b_o_ref[...]
    r = DEEPNORM_ALPHA * x_ref[...] + y
    mu = jnp.mean(r, axis=-1, keepdims=True)
    cen = r - mu
    var = jnp.mean(cen * cen, axis=-1, keepdims=True)
    out_ref[...] = cen * lax.rsqrt(var + LN_EPS) * oln_g_ref[...] + oln_b_ref[...]

    u_hist[:, 0:CONV_PAD, :] = u_hist[:, TILE:U_ROWS, :]
    kt_hist[:, 0:HIST] = kt_hist[:, TILE:TILE + HIST]
    v_hist[0:HIST, :] = v_hist[TILE:TILE + HIST, :]


def _rel_bias_table(rel_bias):
    n = GROUP + KWIN
    diag = np.arange(n)
    idx = np.clip(HIST + (GROUP - 1) - diag, -MAX_REL, MAX_REL) + MAX_REL
    n_far = int(np.sum(idx == 2 * MAX_REL)) - 1
    n_near = int(np.sum(idx == 0)) - 1
    rb = rel_bias.astype(jnp.float32)
    vec = jnp.concatenate([
        jnp.broadcast_to(rb[:, -1:], (N_HEADS, n_far)),
        rb[:, ::-1],
        jnp.broadcast_to(rb[:, :1], (N_HEADS, n_near))], axis=1)
    assert vec.shape == (N_HEADS, n)
    skew = jnp.tile(vec, (1, GROUP))[:, :GROUP * (n - 1)].reshape(N_HEADS, GROUP, n - 1)
    table = skew[:, :, GROUP - 1:GROUP - 1 + KWIN]
    r = np.arange(GROUP)[:, None]
    c = np.arange(KWIN)[None, :]
    band = c // CHUNK - r // CHUNK
    in_band = (band >= 0) & (band <= LEFT_CHUNKS)
    return jnp.where(jnp.asarray(in_band)[None], table * LOG2E, NEG)


def kernel(x, w_in, b_in, conv_w, conv_b, conv_ln_g, conv_ln_b, w_conv_out, rel_bias,
           w_attn_out, w_o, b_o, out_ln_g, out_ln_b):
    bsz, seq, d_model = x.shape
    assert d_model == D_MODEL and seq % TILE == 0 and w_in.shape == (D_MODEL, D_IN)
    f32, bf16 = jnp.float32, jnp.bfloat16

    def row(v):
        return v.astype(f32).reshape(1, -1)

    def full(shape):
        return pl.BlockSpec(shape, lambda b, s: (0,) * len(shape))

    operands = [
        (x, pl.BlockSpec((None, TILE, D_MODEL), lambda b, s: (b, s, 0))),
        (w_in.astype(bf16), full((D_MODEL, D_IN))),
        (row(b_in), full((1, D_IN))),
        (conv_w.astype(f32), full((CONV_WIDTH, D_CONV))),
        (row(conv_b), full((1, D_CONV))),
        (row(conv_ln_g), full((1, D_CONV))),
        (row(conv_ln_b), full((1, D_CONV))),
        (w_conv_out.astype(bf16), full((D_CONV, D_MODEL))),
        (_rel_bias_table(rel_bias), full((N_HEADS, GROUP, KWIN))),
        (w_attn_out.astype(bf16), full((D_ATTN, D_MODEL))),
        (w_o.astype(bf16), full((D_MODEL, D_MODEL))),
        (row(b_o), full((1, D_MODEL))),
        (row(out_ln_g), full((1, D_MODEL))),
        (row(out_ln_b), full((1, D_MODEL))),
    ]
    scratch = [
        pltpu.VMEM((TILE, D_MODEL), bf16),
        pltpu.VMEM((CONV_SLABS, U_ROWS, LANES), f32),
        pltpu.VMEM((CONV_SLABS, TILE, LANES), f32),
        pltpu.VMEM((CONV_SLABS, TILE, LANES), f32),
        pltpu.VMEM((TILE, D_MODEL), bf16),
        pltpu.VMEM((TILE, D_MODEL), bf16),
        pltpu.VMEM((TILE, D_ATTN), bf16),
        pltpu.VMEM((2 * D_ATTN, HIST + TILE), bf16),
        pltpu.VMEM((HIST + TILE, 2 * D_ATTN), bf16),
        pltpu.VMEM((TILE, D_ATTN), bf16),
        pltpu.VMEM((TILE, D_ATTN), bf16),
    ]
    return pl.pallas_call(
        _layer_kernel,
        grid=(bsz, seq // TILE),
        in_specs=[spec for _, spec in operands],
        out_specs=pl.BlockSpec((None, TILE, D_MODEL), lambda b, s: (b, s, 0)),
        out_shape=jax.ShapeDtypeStruct(x.shape, x.dtype),
        scratch_shapes=scratch,
        compiler_params=pltpu.CompilerParams(
            dimension_semantics=("arbitrary", "arbitrary"),
            vmem_limit_bytes=VMEM_LIMIT_BYTES),
        name="hybrid_layer",
    )(*[a for a, _ in operands])
```

```python
import functools
import math

import numpy as np
import jax
import jax.numpy as jnp
from jax import lax
from jax.experimental import pallas as pl
from jax.experimental.pallas import tpu as pltpu

D_MODEL = 1024
CHUNK = 64
LEFT_CHUNKS = 8
D_CONV = D_MODEL // 2
CONV_WIDTH = 31
N_HEADS = 8
HEAD_DIM = 64
D_ATTN = N_HEADS * HEAD_DIM
MAX_REL = 128
LN_EPS = 1e-5
DEPTH = 1
DEEPNORM_ALPHA = (2.0 * DEPTH) ** 0.25

C_VAL, C_GLU, C_GATE = 0, D_CONV, 2 * D_CONV
Q_OFF = 3 * D_CONV
K_OFF = Q_OFF + D_ATTN
V_OFF = K_OFF + D_ATTN
AG_OFF = V_OFF + D_ATTN
GC_OFF = AG_OFF + D_ATTN
GA_OFF = GC_OFF + D_MODEL
D_IN = GA_OFF + D_MODEL

LANES = 128
SUBLANES = 8
MXU_COLS = 256
TILE = 512
GROUP = 4 * CHUNK
HIST = LEFT_CHUNKS * CHUNK
KWIN = HIST + GROUP
CONV_PAD = 32
CONV_ROWS = 32
U_ROWS = CONV_PAD + TILE
CONV_SLABS = D_CONV // LANES
ROW_PHASES = 2
NEG = -1e30
LOG2E = math.log2(math.e)
VMEM_LIMIT_BYTES = 58 * 1024 * 1024

assert TILE == HIST and TILE % GROUP == 0 and CONV_PAD >= CONV_WIDTH - 1
assert CONV_PAD % SUBLANES == 0 and CONV_ROWS % (ROW_PHASES * SUBLANES) == 0


def _sigmoid(x):
    return jax.nn.sigmoid(x)


def _silu(x):
    return x * jax.nn.sigmoid(x)


def _layer_kernel(x_ref, w_in_ref, b_in_ref, conv_w_ref, conv_b_ref, cln_g_ref, cln_b_ref,
                  w_co_ref, bias_ref, w_ao_ref, w_o_ref, b_o_ref, oln_g_ref, oln_b_ref,
                  out_ref,
                  xb_s, u_hist, cg_s, a_conv, gc_s, ga_s, q_s, kt_hist, v_hist, ag_s, o_s):
    f32, bf16 = jnp.float32, jnp.bfloat16
    s_idx = pl.program_id(1)

    @pl.when(s_idx == 0)
    def _():
        u_hist[:, 0:CONV_PAD, :] = jnp.zeros((CONV_SLABS, CONV_PAD, LANES), f32)
        kt_hist[...] = jnp.zeros(kt_hist.shape, bf16)
        v_hist[0:HIST, :] = jnp.zeros((HIST, 2 * D_ATTN), bf16)

    xb_s[...] = x_ref[...].astype(bf16)

    def proj(lo, width):
        z = jnp.dot(xb_s[...], w_in_ref[:, lo:lo + width], preferred_element_type=f32)
        return z + b_in_ref[:, lo:lo + width]

    u = proj(C_VAL, D_CONV) * _sigmoid(proj(C_GLU, D_CONV))
    cg = _silu(proj(C_GATE, D_CONV))
    for c in range(CONV_SLABS):
        u_hist[c, CONV_PAD:U_ROWS, :] = u[:, LANES * c:LANES * (c + 1)]
        cg_s[c] = cg[:, LANES * c:LANES * (c + 1)]

    lane = lax.broadcasted_iota(jnp.int32, (TILE, LANES), 1)
    low = lane < HEAD_DIM

    def q_piece(c):
        q_s[:, c:c + MXU_COLS] = (
            proj(Q_OFF + c, MXU_COLS) * (LOG2E / math.sqrt(HEAD_DIM))).astype(bf16)

    def k_piece(c):
        kv = proj(K_OFF + c, MXU_COLS)
        for jj in range(MXU_COLS // LANES):
            pair_t = kv[:, LANES * jj:LANES * (jj + 1)].T.astype(bf16)
            h0 = 2 * (c // LANES + jj)
            kt_hist[LANES * h0:LANES * h0 + HEAD_DIM, HIST:HIST + TILE] = pair_t[0:HEAD_DIM]
            kt_hist[LANES * (h0 + 1) + HEAD_DIM:LANES * (h0 + 2), HIST:HIST + TILE] = (
                pair_t[HEAD_DIM:LANES])

    def v_piece(c):
        kv = proj(V_OFF + c, MXU_COLS)
        for jj in range(MXU_COLS // LANES):
            pair = kv[:, LANES * jj:LANES * (jj + 1)]
            base = 2 * (c + LANES * jj)
            v_hist[HIST:HIST + TILE, base:base + LANES] = jnp.where(low, pair, 1.0).astype(bf16)
            v_hist[HIST:HIST + TILE, base + LANES:base + 2 * LANES] = (
                jnp.where(low, 1.0, pair).astype(bf16))

    def gate_piece(off, dst, act, c):
        dst[:, c:c + MXU_COLS] = act(proj(off + c, MXU_COLS)).astype(bf16)

    pieces = []
    for c in range(0, D_ATTN, MXU_COLS):
        pieces.append(functools.partial(q_piece, c))
    for c in range(0, D_ATTN, MXU_COLS):
        pieces.append(functools.partial(k_piece, c))
    for c in range(0, D_ATTN, MXU_COLS):
        pieces.append(functools.partial(v_piece, c))
    for c in range(0, D_ATTN, MXU_COLS):
        pieces.append(functools.partial(gate_piece, AG_OFF, ag_s, _silu, c))
    late_pieces = []
    for c in range(0, D_MODEL, MXU_COLS):
        late_pieces.append(functools.partial(gate_piece, GC_OFF, gc_s, _sigmoid, c))
    for c in range(0, D_MODEL, MXU_COLS):
        late_pieces.append(functools.partial(gate_piece, GA_OFF, ga_s, _sigmoid, c))
    n_blocks = TILE // CONV_ROWS
    blocks_per_piece = n_blocks // len(pieces)
    assert blocks_per_piece * len(pieces) == n_blocks

    cln_g = cln_g_ref[...]
    cln_b = cln_b_ref[...]
    row0 = CONV_PAD - (CONV_WIDTH - 1)
    span = ROW_PHASES * SUBLANES
    def conv_block(rb):
        r0 = rb * CONV_ROWS
        sets = [(a, p) for a in range(CONV_ROWS // span) for p in range(ROW_PHASES)]

        def rows_of(a, p, shift=0):
            return pl.ds(shift + r0 + span * a + p, SUBLANES, stride=ROW_PHASES)

        acc = [[jnp.broadcast_to(conv_b_ref[:, LANES * c:LANES * (c + 1)], (SUBLANES, LANES))
                for c in range(CONV_SLABS)] for _ in sets]
        loaded = {}

        def taps(c, a, p, k):
            key = (c, a, p + k)
            if key not in loaded:
                loaded[key] = u_hist[c, rows_of(a, p, row0 + k), :]
            return loaded[key]

        for k in range(CONV_WIDTH):
            for c in range(CONV_SLABS):
                w_kc = conv_w_ref[k:k + 1, LANES * c:LANES * (c + 1)]
                for i, (a, p) in enumerate(sets):
                    acc[i][c] = acc[i][c] + w_kc * taps(c, a, p, k)
        conv = jnp.concatenate([jnp.concatenate(acc_i, axis=1) for acc_i in acc], axis=0)
        gate = jnp.concatenate(
            [jnp.concatenate([cg_s[c, rows_of(a, p), :] for c in range(CONV_SLABS)], axis=1)
             for a, p in sets], axis=0)
        mu = jnp.mean(conv, axis=-1, keepdims=True)
        cen = conv - mu
        var = jnp.mean(cen * cen, axis=-1, keepdims=True)
        y = cen * lax.rsqrt(var + LN_EPS) * cln_g + cln_b
        act = _silu(y) * gate
        for i, (a, p) in enumerate(sets):
            for c in range(CONV_SLABS):
                a_conv[c, rows_of(a, p), :] = act[SUBLANES * i:SUBLANES * (i + 1),
                                                  LANES * c:LANES * (c + 1)]

    for n, piece in enumerate(pieces):
        piece()
        for rb in range(blocks_per_piece * n, blocks_per_piece * (n + 1)):
            conv_block(rb)

    first = s_idx == 0
    off_first = jnp.where(first, jnp.float32(-NEG), jnp.float32(0.0))
    lane_g = lax.broadcasted_iota(jnp.int32, (GROUP, LANES), 1)
    low_g = lane_g < HEAD_DIM
    col_tiles = KWIN // LANES
    band_tiles = (LEFT_CHUNKS + 1) * CHUNK // LANES + 1
    zero_tile = jnp.zeros((CHUNK, LANES), bf16)
    late_every = (TILE // GROUP) * N_HEADS // len(late_pieces)
    for g in range(TILE // GROUP):
        rows = slice(GROUP * g, GROUP * (g + 1))
        win = slice(GROUP * g, GROUP * g + KWIN)
        n_pre = (HIST - GROUP * g) // GROUP
        offs = [off_first if t < n_pre else None for t in range(KWIN // GROUP)]
        for j in range(N_HEADS // 2):
            qp = q_s[rows, LANES * j:LANES * (j + 1)]
            o_heads = []
            for hh in range(2):
                h = 2 * j + hh
                kh_t = kt_hist[LANES * h:LANES * (h + 1), win]
                s = jnp.dot(qp, kh_t, preferred_element_type=f32)
                if late_pieces and h % late_every == 0:
                    late_pieces.pop(0)()
                e_rows = []
                for ci in range(GROUP // CHUNK):
                    r = slice(CHUNK * ci, CHUNK * (ci + 1))
                    first_tile = (CHUNK * ci) // LANES
                    live = range(first_tile, first_tile + band_tiles)
                    sc = {t: s[r, LANES * t:LANES * (t + 1)] + bias_ref[h, r, LANES * t:LANES * (t + 1)]
                          for t in live}
                    tops = [sc[t] if offs[LANES * t // GROUP] is None
                            else sc[t] - offs[LANES * t // GROUP] for t in live]
                    m = jnp.max(functools.reduce(jnp.maximum, tops), axis=-1, keepdims=True)
                    e_tiles = []
                    for t in range(col_tiles):
                        if t in live:
                            off = offs[LANES * t // GROUP]
                            mt = m if off is None else m + off
                            e_tiles.append(jnp.exp2(sc[t] - mt).astype(bf16))
                        else:
                            e_tiles.append(zero_tile)
                    e_rows.append(jnp.concatenate(e_tiles, axis=1))
                e = jnp.concatenate(e_rows, axis=0)
                vh = v_hist[win, LANES * h:LANES * (h + 1)]
                o_heads.append(jnp.dot(e, vh, preferred_element_type=f32))
            num = jnp.where(low_g, o_heads[0], o_heads[1])
            den = jnp.where(low_g, pltpu.roll(o_heads[0], HEAD_DIM, 1),
                            pltpu.roll(o_heads[1], HEAD_DIM, 1))
            o_s[rows, LANES * j:LANES * (j + 1)] = (
                num / den * ag_s[rows, LANES * j:LANES * (j + 1)]).astype(bf16)

    assert not late_pieces
    attn_out = jnp.dot(o_s[...], w_ao_ref[...], preferred_element_type=f32)
    a_all = jnp.concatenate([a_conv[c] for c in range(CONV_SLABS)], axis=1).astype(bf16)
    conv_out = jnp.dot(a_all, w_co_ref[...], preferred_element_type=f32)
    h = gc_s[...] * conv_out.astype(bf16) + ga_s[...] * attn_out.astype(bf16)
    y = jnp.dot(h, w_o_ref[...], preferred_element_type=f32) + b_o_ref[...]
    r = DEEPNORM_ALPHA * x_ref[...] + y
    mu = jnp.mean(r, axis=-1, keepdims=True)
    cen = r - mu
    var = jnp.mean(cen * cen, axis=-1, keepdims=True)
    out_ref[...] = cen * lax.rsqrt(var + LN_EPS) * oln_g_ref[...] + oln_b_ref[...]

    u_hist[:, 0:CONV_PAD, :] = u_hist[:, TILE:U_ROWS, :]
    kt_hist[:, 0:HIST] = kt_hist[:, TILE:TILE + HIST]
    v_hist[0:HIST, :] = v_hist[TILE:TILE + HIST, :]


def _rel_bias_table(rel_bias):
    n = GROUP + KWIN
    diag = np.arange(n)
    idx = np.clip(HIST + (GROUP - 1) - diag, -MAX_REL, MAX_REL) + MAX_REL
    n_far = int(np.sum(idx == 2 * MAX_REL)) - 1
    n_near = int(np.sum(idx == 0)) - 1
    rb = rel_bias.astype(jnp.float32)
    vec = jnp.concatenate([
        jnp.broadcast_to(rb[:, -1:], (N_HEADS, n_far)),
        rb[:, ::-1],
        jnp.broadcast_to(rb[:, :1], (N_HEADS, n_near))], axis=1)
    assert vec.shape == (N_HEADS, n)
    skew = jnp.tile(vec, (1, GROUP))[:, :GROUP * (n - 1)].reshape(N_HEADS, GROUP, n - 1)
    table = skew[:, :, GROUP - 1:GROUP - 1 + KWIN]
    r = np.arange(GROUP)[:, None]
    c = np.arange(KWIN)[None, :]
    band = c // CHUNK - r // CHUNK
    in_band = (band >= 0) & (band <= LEFT_CHUNKS)
    return jnp.where(jnp.asarray(in_band)[None], table * LOG2E, NEG)


def kernel(x, w_in, b_in, conv_w, conv_b, conv_ln_g, conv_ln_b, w_conv_out, rel_bias,
           w_attn_out, w_o, b_o, out_ln_g, out_ln_b):
    bsz, seq, d_model = x.shape
    assert d_model == D_MODEL and seq % TILE == 0 and w_in.shape == (D_MODEL, D_IN)
    f32, bf16 = jnp.float32, jnp.bfloat16

    def row(v):
        return v.astype(f32).reshape(1, -1)

    def full(shape):
        return pl.BlockSpec(shape, lambda b, s: (0,) * len(shape))

    operands = [
        (x, pl.BlockSpec((None, TILE, D_MODEL), lambda b, s: (b, s, 0))),
        (w_in.astype(bf16), full((D_MODEL, D_IN))),
        (row(b_in), full((1, D_IN))),
        (conv_w.astype(f32), full((CONV_WIDTH, D_CONV))),
        (row(conv_b), full((1, D_CONV))),
        (row(conv_ln_g), full((1, D_CONV))),
        (row(conv_ln_b), full((1, D_CONV))),
        (w_conv_out.astype(bf16), full((D_CONV, D_MODEL))),
        (_rel_bias_table(rel_bias), full((N_HEADS, GROUP, KWIN))),
        (w_attn_out.astype(bf16), full((D_ATTN, D_MODEL))),
        (w_o.astype(bf16), full((D_MODEL, D_MODEL))),
        (row(b_o), full((1, D_MODEL))),
        (row(out_ln_g), full((1, D_MODEL))),
        (row(out_ln_b), full((1, D_MODEL))),
    ]
    scratch = [
        pltpu.VMEM((TILE, D_MODEL), bf16),
        pltpu.VMEM((CONV_SLABS, U_ROWS, LANES), f32),
        pltpu.VMEM((CONV_SLABS, TILE, LANES), f32),
        pltpu.VMEM((CONV_SLABS, TILE, LANES), f32),
        pltpu.VMEM((TILE, D_MODEL), bf16),
        pltpu.VMEM((TILE, D_MODEL), bf16),
        pltpu.VMEM((TILE, D_ATTN), bf16),
        pltpu.VMEM((2 * D_ATTN, HIST + TILE), bf16),
        pltpu.VMEM((HIST + TILE, 2 * D_ATTN), bf16),
        pltpu.VMEM((TILE, D_ATTN), bf16),
        pltpu.VMEM((TILE, D_ATTN), bf16),
    ]
    return pl.pallas_call(
        _layer_kernel,
        grid=(bsz, seq // TILE),
        in_specs=[spec for _, spec in operands],
        out_specs=pl.BlockSpec((None, TILE, D_MODEL), lambda b, s: (b, s, 0)),
        out_shape=jax.ShapeDtypeStruct(x.shape, x.dtype),
        scratch_shapes=scratch,
        compiler_params=pltpu.CompilerParams(
            dimension_semantics=("arbitrary", "arbitrary"),
            vmem_limit_bytes=VMEM_LIMIT_BYTES),
        name="hybrid_layer",
    )(*[a for a, _ in operands])
```

```python
import functools
import math

import numpy as np
import jax
import jax.numpy as jnp
from jax import lax
from jax.experimental import pallas as pl
from jax.experimental.pallas import tpu as pltpu

D_MODEL = 1024
CHUNK = 64
LEFT_CHUNKS = 8
D_CONV = D_MODEL // 2
CONV_WIDTH = 31
N_HEADS = 8
HEAD_DIM = 64
D_ATTN = N_HEADS * HEAD_DIM
MAX_REL = 128
LN_EPS = 1e-5
DEPTH = 1
DEEPNORM_ALPHA = (2.0 * DEPTH) ** 0.25

C_VAL, C_GLU, C_GATE = 0, D_CONV, 2 * D_CONV
Q_OFF = 3 * D_CONV
K_OFF = Q_OFF + D_ATTN
V_OFF = K_OFF + D_ATTN
AG_OFF = V_OFF + D_ATTN
GC_OFF = AG_OFF + D_ATTN
GA_OFF = GC_OFF + D_MODEL
D_IN = GA_OFF + D_MODEL

LANES = 128
SUBLANES = 8
MXU_COLS = 256
TILE = 512
GROUP = 4 * CHUNK
HIST = LEFT_CHUNKS * CHUNK
KWIN = HIST + GROUP
CONV_PAD = 32
CONV_ROWS = 32
U_ROWS = CONV_PAD + TILE
CONV_SLABS = D_CONV // LANES
ROW_PHASES = 2
NEG = -1e30
LOG2E = math.log2(math.e)
VMEM_LIMIT_BYTES = 58 * 1024 * 1024

assert TILE == HIST and TILE % GROUP == 0 and CONV_PAD >= CONV_WIDTH - 1
assert CONV_PAD % SUBLANES == 0 and CONV_ROWS % (ROW_PHASES * SUBLANES) == 0


def _sigmoid(x):
    return jax.nn.sigmoid(x)


def _silu(x):
    return x * jax.nn.sigmoid(x)


def _layer_kernel(x_ref, w_in_ref, b_in_ref, conv_w_ref, conv_b_ref, cln_g_ref, cln_b_ref,
                  w_co_ref, bias_ref, w_ao_ref, w_o_ref, b_o_ref, oln_g_ref, oln_b_ref,
                  out_ref,
                  xb_s, u_hist, cg_s, a_conv, gc_s, ga_s, q_s, kt_hist, v_hist, ag_s, o_s):
    f32, bf16 = jnp.float32, jnp.bfloat16
    s_idx = pl.program_id(1)

    @pl.when(s_idx == 0)
    def _():
        u_hist[:, 0:CONV_PAD, :] = jnp.zeros((CONV_SLABS, CONV_PAD, LANES), f32)
        kt_hist[...] = jnp.zeros(kt_hist.shape, bf16)
        v_hist[0:HIST, :] = jnp.zeros((HIST, 2 * D_ATTN), bf16)

    xb_s[...] = x_ref[...].astype(bf16)

    def proj(lo, width):
        z = jnp.dot(xb_s[...], w_in_ref[:, lo:lo + width], preferred_element_type=f32)
        return z + b_in_ref[:, lo:lo + width]

    u = proj(C_VAL, D_CONV) * _sigmoid(proj(C_GLU, D_CONV))
    cg = _silu(proj(C_GATE, D_CONV))
    for c in range(CONV_SLABS):
        u_hist[c, CONV_PAD:U_ROWS, :] = u[:, LANES * c:LANES * (c + 1)]
        cg_s[c] = cg[:, LANES * c:LANES * (c + 1)]

    lane = lax.broadcasted_iota(jnp.int32, (TILE, LANES), 1)
    low = lane < HEAD_DIM

    def q_piece(c):
        q_s[:, c:c + MXU_COLS] = (
            proj(Q_OFF + c, MXU_COLS) * (LOG2E / math.sqrt(HEAD_DIM))).astype(bf16)

    def k_piece(c):
        kv = proj(K_OFF + c, MXU_COLS)
        for jj in range(MXU_COLS // LANES):
            pair_t = kv[:, LANES * jj:LANES * (jj + 1)].T.astype(bf16)
            h0 = 2 * (c // LANES + jj)
            kt_hist[LANES * h0:LANES * h0 + HEAD_DIM, HIST:HIST + TILE] = pair_t[0:HEAD_DIM]
            kt_hist[LANES * (h0 + 1) + HEAD_DIM:LANES * (h0 + 2), HIST:HIST + TILE] = (
                pair_t[HEAD_DIM:LANES])

    def v_piece(c):
        kv = proj(V_OFF + c, MXU_COLS)
        for jj in range(MXU_COLS // LANES):
            pair = kv[:, LANES * jj:LANES * (jj + 1)]
            base = 2 * (c + LANES * jj)
            v_hist[HIST:HIST + TILE, base:base + LANES] = jnp.where(low, pair, 1.0).astype(bf16)
            v_hist[HIST:HIST + TILE, base + LANES:base + 2 * LANES] = (
                jnp.where(low, 1.0, pair).astype(bf16))

    def gate_piece(off, dst, act, c):
        dst[:, c:c + MXU_COLS] = act(proj(off + c, MXU_COLS)).astype(bf16)

    pieces = []
    for c in range(0, D_ATTN, MXU_COLS):
        pieces.append(functools.partial(q_piece, c))
    for c in range(0, D_ATTN, MXU_COLS):
        pieces.append(functools.partial(k_piece, c))
    for c in range(0, D_ATTN, MXU_COLS):
        pieces.append(functools.partial(v_piece, c))
    for c in range(0, D_ATTN, MXU_COLS):
        pieces.append(functools.partial(gate_piece, AG_OFF, ag_s, _silu, c))
    late_pieces = []
    for c in range(0, D_MODEL, MXU_COLS):
        late_pieces.append(functools.partial(gate_piece, GC_OFF, gc_s, _sigmoid, c))
    for c in range(0, D_MODEL, MXU_COLS):
        late_pieces.append(functools.partial(gate_piece, GA_OFF, ga_s, _sigmoid, c))
    n_blocks = TILE // CONV_ROWS
    n_heads_total = (TILE // GROUP) * N_HEADS
    blocks_per_head = n_blocks // n_heads_total
    assert blocks_per_head * n_heads_total == n_blocks

    cln_g = cln_g_ref[...]
    cln_b = cln_b_ref[...]
    row0 = CONV_PAD - (CONV_WIDTH - 1)
    span = ROW_PHASES * SUBLANES
    def conv_block(rb):
        r0 = rb * CONV_ROWS
        sets = [(a, p) for a in range(CONV_ROWS // span) for p in range(ROW_PHASES)]

        def rows_of(a, p, shift=0):
            return pl.ds(shift + r0 + span * a + p, SUBLANES, stride=ROW_PHASES)

        acc = [[jnp.broadcast_to(conv_b_ref[:, LANES * c:LANES * (c + 1)], (SUBLANES, LANES))
                for c in range(CONV_SLABS)] for _ in sets]
        loaded = {}

        def taps(c, a, p, k):
            key = (c, a, p + k)
            if key not in loaded:
                loaded[key] = u_hist[c, rows_of(a, p, row0 + k), :]
            return loaded[key]

        for k in range(CONV_WIDTH):
            for c in range(CONV_SLABS):
                w_kc = conv_w_ref[k:k + 1, LANES * c:LANES * (c + 1)]
                for i, (a, p) in enumerate(sets):
                    acc[i][c] = acc[i][c] + w_kc * taps(c, a, p, k)
        conv = jnp.concatenate([jnp.concatenate(acc_i, axis=1) for acc_i in acc], axis=0)
        gate = jnp.concatenate(
            [jnp.concatenate([cg_s[c, rows_of(a, p), :] for c in range(CONV_SLABS)], axis=1)
             for a, p in sets], axis=0)
        mu = jnp.mean(conv, axis=-1, keepdims=True)
        cen = conv - mu
        var = jnp.mean(cen * cen, axis=-1, keepdims=True)
        y = cen * lax.rsqrt(var + LN_EPS) * cln_g + cln_b
        act = _silu(y) * gate
        for i, (a, p) in enumerate(sets):
            for c in range(CONV_SLABS):
                a_conv[c, rows_of(a, p), :] = act[SUBLANES * i:SUBLANES * (i + 1),
                                                  LANES * c:LANES * (c + 1)]
        return act[0:SUBLANES, 0:LANES] * 0.0

    for piece in pieces:
        piece()

    first = s_idx == 0
    off_first = jnp.where(first, jnp.float32(-NEG), jnp.float32(0.0))
    lane_g = lax.broadcasted_iota(jnp.int32, (GROUP, LANES), 1)
    low_g = lane_g < HEAD_DIM
    col_tiles = KWIN // LANES
    band_tiles = (LEFT_CHUNKS + 1) * CHUNK // LANES + 1
    zero_tile = jnp.zeros((CHUNK, LANES), bf16)
    late_every = n_heads_total // len(late_pieces)
    for g in range(TILE // GROUP):
        rows = slice(GROUP * g, GROUP * (g + 1))
        win = slice(GROUP * g, GROUP * g + KWIN)
        n_pre = (HIST - GROUP * g) // GROUP
        offs = [off_first if t < n_pre else None for t in range(KWIN // GROUP)]
        for j in range(N_HEADS // 2):
            qp = q_s[rows, LANES * j:LANES * (j + 1)]
            o_heads = []
            for hh in range(2):
                h = 2 * j + hh
                kh_t = kt_hist[LANES * h:LANES * (h + 1), win]
                s = jnp.dot(qp, kh_t, preferred_element_type=f32)
                if late_pieces and h % late_every == 0:
                    late_pieces.pop(0)()
                head_no = N_HEADS * g + h
                conv_zero = sum(conv_block(rb) for rb in range(blocks_per_head * head_no,
                                                               blocks_per_head * (head_no + 1)))
                e_rows = []
                for ci in range(GROUP // CHUNK):
                    r = slice(CHUNK * ci, CHUNK * (ci + 1))
                    first_tile = (CHUNK * ci) // LANES
                    live = range(first_tile, first_tile + band_tiles)
                    sc = {t: s[r, LANES * t:LANES * (t + 1)] + bias_ref[h, r, LANES * t:LANES * (t + 1)]
                          for t in live}
                    tops = [sc[t] if offs[LANES * t // GROUP] is None
                            else sc[t] - offs[LANES * t // GROUP] for t in live]
                    if ci == 0:
                        tops[-1] = tops[-1] + jnp.concatenate([conv_zero] * (CHUNK // SUBLANES), axis=0)
                    m = jnp.max(functools.reduce(jnp.maximum, tops), axis=-1, keepdims=True)
                    e_tiles = []
                    for t in range(col_tiles):
                        if t in live:
                            off = offs[LANES * t // GROUP]
                            mt = m if off is None else m + off
                            e_tiles.append(jnp.exp2(sc[t] - mt).astype(bf16))
                        else:
                            e_tiles.append(zero_tile)
                    e_rows.append(jnp.concatenate(e_tiles, axis=1))
                e = jnp.concatenate(e_rows, axis=0)
                vh = v_hist[win, LANES * h:LANES * (h + 1)]
                o_heads.append(jnp.dot(e, vh, preferred_element_type=f32))
            num = jnp.where(low_g, o_heads[0], o_heads[1])
            den = jnp.where(low_g, pltpu.roll(o_heads[0], HEAD_DIM, 1),
                            pltpu.roll(o_heads[1], HEAD_DIM, 1))
            o_s[rows, LANES * j:LANES * (j + 1)] = (
                num / den * ag_s[rows, LANES * j:LANES * (j + 1)]).astype(bf16)

    assert not late_pieces
    attn_out = jnp.dot(o_s[...], w_ao_ref[...], preferred_element_type=f32)
    a_all = jnp.concatenate([a_conv[c] for c in range(CONV_SLABS)], axis=1).astype(bf16)
    conv_out = jnp.dot(a_all, w_co_ref[...], preferred_element_type=f32)
    h = gc_s[...] * conv_out.astype(bf16) + ga_s[...] * attn_out.astype(bf16)
    y = jnp.dot(h, w_o_ref[...], preferred_element_type=f32) + b_o_ref[...]
    r = DEEPNORM_ALPHA * x_ref[...] + y
    mu = jnp.mean(r, axis=-1, keepdims=True)
    cen = r - mu
    var = jnp.mean(cen * cen, axis=-1, keepdims=True)
    out_ref[...] = cen * lax.rsqrt(var + LN_EPS) * oln_g_ref[...] + oln_b_ref[...]

    u_hist[:, 0:CONV_PAD, :] = u_hist[:, TILE:U_ROWS, :]
    kt_hist[:, 0:HIST] = kt_hist[:, TILE:TILE + HIST]
    v_hist[0:HIST, :] = v_hist[TILE:TILE + HIST, :]


def _rel_bias_table(rel_bias):
    n = GROUP + KWIN
    diag = np.arange(n)
    idx = np.clip(HIST + (GROUP - 1) - diag, -MAX_REL, MAX_REL) + MAX_REL
    n_far = int(np.sum(idx == 2 * MAX_REL)) - 1
    n_near = int(np.sum(idx == 0)) - 1
    rb = rel_bias.astype(jnp.float32)
    vec = jnp.concatenate([
        jnp.broadcast_to(rb[:, -1:], (N_HEADS, n_far)),
        rb[:, ::-1],
        jnp.broadcast_to(rb[:, :1], (N_HEADS, n_near))], axis=1)
    assert vec.shape == (N_HEADS, n)
    skew = jnp.tile(vec, (1, GROUP))[:, :GROUP * (n - 1)].reshape(N_HEADS, GROUP, n - 1)
    table = skew[:, :, GROUP - 1:GROUP - 1 + KWIN]
    r = np.arange(GROUP)[:, None]
    c = np.arange(KWIN)[None, :]
    band = c // CHUNK - r // CHUNK
    in_band = (band >= 0) & (band <= LEFT_CHUNKS)
    return jnp.where(jnp.asarray(in_band)[None], table * LOG2E, NEG)


def kernel(x, w_in, b_in, conv_w, conv_b, conv_ln_g, conv_ln_b, w_conv_out, rel_bias,
           w_attn_out, w_o, b_o, out_ln_g, out_ln_b):
    bsz, seq, d_model = x.shape
    assert d_model == D_MODEL and seq % TILE == 0 and w_in.shape == (D_MODEL, D_IN)
    f32, bf16 = jnp.float32, jnp.bfloat16

    def row(v):
        return v.astype(f32).reshape(1, -1)

    def full(shape):
        return pl.BlockSpec(shape, lambda b, s: (0,) * len(shape))

    operands = [
        (x, pl.BlockSpec((None, TILE, D_MODEL), lambda b, s: (b, s, 0))),
        (w_in.astype(bf16), full((D_MODEL, D_IN))),
        (row(b_in), full((1, D_IN))),
        (conv_w.astype(f32), full((CONV_WIDTH, D_CONV))),
        (row(conv_b), full((1, D_CONV))),
        (row(conv_ln_g), full((1, D_CONV))),
        (row(conv_ln_b), full((1, D_CONV))),
        (w_conv_out.astype(bf16), full((D_CONV, D_MODEL))),
        (_rel_bias_table(rel_bias), full((N_HEADS, GROUP, KWIN))),
        (w_attn_out.astype(bf16), full((D_ATTN, D_MODEL))),
        (w_o.astype(bf16), full((D_MODEL, D_MODEL))),
        (row(b_o), full((1, D_MODEL))),
        (row(out_ln_g), full((1, D_MODEL))),
        (row(out_ln_b), full((1, D_MODEL))),
    ]
    scratch = [
        pltpu.VMEM((TILE, D_MODEL), bf16),
        pltpu.VMEM((CONV_SLABS, U_ROWS, LANES), f32),
        pltpu.VMEM((CONV_SLABS, TILE, LANES), f32),
        pltpu.VMEM((CONV_SLABS, TILE, LANES), f32),
        pltpu.VMEM((TILE, D_MODEL), bf16),
        pltpu.VMEM((TILE, D_MODEL), bf16),
        pltpu.VMEM((TILE, D_ATTN), bf16),
        pltpu.VMEM((2 * D_ATTN, HIST + TILE), bf16),
        pltpu.VMEM((HIST + TILE, 2 * D_ATTN), bf16),
        pltpu.VMEM((TILE, D_ATTN), bf16),
        pltpu.VMEM((TILE, D_ATTN), bf16),
    ]
    return pl.pallas_call(
        _layer_kernel,
        grid=(bsz, seq // TILE),
        in_specs=[spec for _, spec in operands],
        out_specs=pl.BlockSpec((None, TILE, D_MODEL), lambda b, s: (b, s, 0)),
        out_shape=jax.ShapeDtypeStruct(x.shape, x.dtype),
        scratch_shapes=scratch,
        compiler_params=pltpu.CompilerParams(
            dimension_semantics=("arbitrary", "arbitrary"),
            vmem_limit_bytes=VMEM_LIMIT_BYTES),
        name="hybrid_layer",
    )(*[a for a, _ in operands])
```

```python
import functools
import math

import numpy as np
import jax
import jax.numpy as jnp
from jax import lax
from jax.experimental import pallas as pl
from jax.experimental.pallas import tpu as pltpu

D_MODEL = 1024
CHUNK = 64
LEFT_CHUNKS = 8
D_CONV = D_MODEL // 2
CONV_WIDTH = 31
N_HEADS = 8
HEAD_DIM = 64
D_ATTN = N_HEADS * HEAD_DIM
MAX_REL = 128
LN_EPS = 1e-5
DEPTH = 1
DEEPNORM_ALPHA = (2.0 * DEPTH) ** 0.25

C_VAL, C_GLU, C_GATE = 0, D_CONV, 2 * D_CONV
Q_OFF = 3 * D_CONV
K_OFF = Q_OFF + D_ATTN
V_OFF = K_OFF + D_ATTN
AG_OFF = V_OFF + D_ATTN
GC_OFF = AG_OFF + D_ATTN
GA_OFF = GC_OFF + D_MODEL
D_IN = GA_OFF + D_MODEL

LANES = 128
SUBLANES = 8
MXU_COLS = 256
TILE = 512
GROUP = 4 * CHUNK
HIST = LEFT_CHUNKS * CHUNK
KWIN = HIST + GROUP
CONV_PAD = 32
CONV_ROWS = 32
U_ROWS = CONV_PAD + TILE
CONV_SLABS = D_CONV // LANES
ROW_PHASES = 2
NEG = -1e30
LOG2E = math.log2(math.e)
VMEM_LIMIT_BYTES = 58 * 1024 * 1024

assert TILE == HIST and TILE % GROUP == 0 and CONV_PAD >= CONV_WIDTH - 1
assert CONV_PAD % SUBLANES == 0 and CONV_ROWS % (ROW_PHASES * SUBLANES) == 0


def _sigmoid(x):
    return jax.nn.sigmoid(x)


def _silu(x):
    return x * jax.nn.sigmoid(x)


def _layer_kernel(x_ref, w_in_ref, b_in_ref, conv_w_ref, conv_b_ref, cln_g_ref, cln_b_ref,
                  w_co_ref, bias_ref, w_ao_ref, w_o_ref, b_o_ref, oln_g_ref, oln_b_ref,
                  out_ref,
                  xb_s, u_hist, cg_s, a_conv, gc_s, ga_s, q_s, kt_hist, v_hist, ag_s, o_s):
    f32, bf16 = jnp.float32, jnp.bfloat16
    s_idx = pl.program_id(1)

    @pl.when(s_idx == 0)
    def _():
        u_hist[:, 0:CONV_PAD, :] = jnp.zeros((CONV_SLABS, CONV_PAD, LANES), f32)
        kt_hist[...] = jnp.zeros(kt_hist.shape, bf16)
        v_hist[0:HIST, :] = jnp.zeros((HIST, 2 * D_ATTN), bf16)

    xb_s[...] = x_ref[...].astype(bf16)

    def proj(lo, width):
        z = jnp.dot(xb_s[...], w_in_ref[:, lo:lo + width], preferred_element_type=f32)
        return z + b_in_ref[:, lo:lo + width]

    u = proj(C_VAL, D_CONV) * _sigmoid(proj(C_GLU, D_CONV))
    cg = _silu(proj(C_GATE, D_CONV))
    for c in range(CONV_SLABS):
        u_hist[c, CONV_PAD:U_ROWS, :] = u[:, LANES * c:LANES * (c + 1)]
        cg_s[c] = cg[:, LANES * c:LANES * (c + 1)]

    lane = lax.broadcasted_iota(jnp.int32, (TILE, LANES), 1)
    low = lane < HEAD_DIM

    def q_piece(c):
        q_s[:, c:c + MXU_COLS] = (
            proj(Q_OFF + c, MXU_COLS) * (LOG2E / math.sqrt(HEAD_DIM))).astype(bf16)

    def k_piece(c):
        kv = proj(K_OFF + c, MXU_COLS)
        for jj in range(MXU_COLS // LANES):
            pair_t = kv[:, LANES * jj:LANES * (jj + 1)].T.astype(bf16)
            h0 = 2 * (c // LANES + jj)
            kt_hist[LANES * h0:LANES * h0 + HEAD_DIM, HIST:HIST + TILE] = pair_t[0:HEAD_DIM]
            kt_hist[LANES * (h0 + 1) + HEAD_DIM:LANES * (h0 + 2), HIST:HIST + TILE] = (
                pair_t[HEAD_DIM:LANES])

    def v_piece(c):
        kv = proj(V_OFF + c, MXU_COLS)
        for jj in range(MXU_COLS // LANES):
            pair = kv[:, LANES * jj:LANES * (jj + 1)]
            base = 2 * (c + LANES * jj)
            v_hist[HIST:HIST + TILE, base:base + LANES] = jnp.where(low, pair, 1.0).astype(bf16)
            v_hist[HIST:HIST + TILE, base + LANES:base + 2 * LANES] = (
                jnp.where(low, 1.0, pair).astype(bf16))

    def gate_piece(off, dst, act, c):
        dst[:, c:c + MXU_COLS] = act(proj(off + c, MXU_COLS)).astype(bf16)

    pieces = []
    for c in range(0, D_ATTN, MXU_COLS):
        pieces.append(functools.partial(q_piece, c))
    for c in range(0, D_ATTN, MXU_COLS):
        pieces.append(functools.partial(k_piece, c))
    for c in range(0, D_ATTN, MXU_COLS):
        pieces.append(functools.partial(v_piece, c))
    for c in range(0, D_ATTN, MXU_COLS):
        pieces.append(functools.partial(gate_piece, AG_OFF, ag_s, _silu, c))
    late_pieces = []
    for c in range(0, D_MODEL, MXU_COLS):
        late_pieces.append(functools.partial(gate_piece, GC_OFF, gc_s, _sigmoid, c))
    for c in range(0, D_MODEL, MXU_COLS):
        late_pieces.append(functools.partial(gate_piece, GA_OFF, ga_s, _sigmoid, c))
    n_blocks = TILE // CONV_ROWS
    blocks_per_piece = n_blocks // len(pieces)
    assert blocks_per_piece * len(pieces) == n_blocks

    cln_g = cln_g_ref[...]
    cln_b = cln_b_ref[...]
    row0 = CONV_PAD - (CONV_WIDTH - 1)
    span = ROW_PHASES * SUBLANES

    def conv_block(rb):
        r0 = rb * CONV_ROWS
        sets = [(a, p) for a in range(CONV_ROWS // span) for p in range(ROW_PHASES)]

        def rows_of(a, p, shift=0):
            return pl.ds(shift + r0 + span * a + p, SUBLANES, stride=ROW_PHASES)

        acc = [[jnp.broadcast_to(conv_b_ref[:, LANES * c:LANES * (c + 1)], (SUBLANES, LANES))
                for c in range(CONV_SLABS)] for _ in sets]
        loaded = {}

        def taps(c, a, p, k):
            key = (c, a, p + k)
            if key not in loaded:
                loaded[key] = u_hist[c, rows_of(a, p, row0 + k), :]
            return loaded[key]

        for k in range(CONV_WIDTH):
            for c in range(CONV_SLABS):
                w_kc = conv_w_ref[k:k + 1, LANES * c:LANES * (c + 1)]
                for i, (a, p) in enumerate(sets):
                    acc[i][c] = acc[i][c] + w_kc * taps(c, a, p, k)
        conv = jnp.concatenate([jnp.concatenate(acc_i, axis=1) for acc_i in acc], axis=0)
        gate = jnp.concatenate(
            [jnp.concatenate([cg_s[c, rows_of(a, p), :] for c in range(CONV_SLABS)], axis=1)
             for a, p in sets], axis=0)
        mu = jnp.mean(conv, axis=-1, keepdims=True)
        cen = conv - mu
        var = jnp.mean(cen * cen, axis=-1, keepdims=True)
        y = cen * lax.rsqrt(var + LN_EPS) * cln_g + cln_b
        act = _silu(y) * gate
        for i, (a, p) in enumerate(sets):
            for c in range(CONV_SLABS):
                a_conv[c, rows_of(a, p), :] = act[SUBLANES * i:SUBLANES * (i + 1),
                                                  LANES * c:LANES * (c + 1)]

    for n, piece in enumerate(pieces):
        piece()
        for rb in range(blocks_per_piece * n, blocks_per_piece * (n + 1)):
            conv_block(rb)

    first = s_idx == 0
    off_first = jnp.where(first, jnp.float32(-NEG), jnp.float32(0.0))
    lane_g = lax.broadcasted_iota(jnp.int32, (GROUP, LANES), 1)
    low_g = lane_g < HEAD_DIM
    col_tiles = KWIN // LANES
    band_tiles = (LEFT_CHUNKS + 1) * CHUNK // LANES + 1
    zero_tile = jnp.zeros((CHUNK, LANES), bf16)

    def scores(g, h):
        qp = q_s[GROUP * g:GROUP * (g + 1), LANES * (h // 2):LANES * (h // 2 + 1)]
        kh_t = kt_hist[LANES * h:LANES * (h + 1), GROUP * g:GROUP * g + KWIN]
        return jnp.dot(qp, kh_t, preferred_element_type=f32)

    def head_output(g, h, s):
        n_pre = (HIST - GROUP * g) // GROUP
        offs = [off_first if t < n_pre else None for t in range(KWIN // GROUP)]
        e_rows = []
        for ci in range(GROUP // CHUNK):
            r = slice(CHUNK * ci, CHUNK * (ci + 1))
            first_tile = (CHUNK * ci) // LANES
            live = range(first_tile, first_tile + band_tiles)
            sc = {t: s[r, LANES * t:LANES * (t + 1)] + bias_ref[h, r, LANES * t:LANES * (t + 1)]
                  for t in live}
            tops = [sc[t] if offs[LANES * t // GROUP] is None
                    else sc[t] - offs[LANES * t // GROUP] for t in live]
            m = jnp.max(functools.reduce(jnp.maximum, tops), axis=-1, keepdims=True)
            e_tiles = []
            for t in range(col_tiles):
                if t in live:
                    off = offs[LANES * t // GROUP]
                    mt = m if off is None else m + off
                    e_tiles.append(jnp.exp2(sc[t] - mt).astype(bf16))
                else:
                    e_tiles.append(zero_tile)
            e_rows.append(jnp.concatenate(e_tiles, axis=1))
        e = jnp.concatenate(e_rows, axis=0)
        vh = v_hist[GROUP * g:GROUP * g + KWIN, LANES * h:LANES * (h + 1)]
        return jnp.dot(e, vh, preferred_element_type=f32)

    heads = [(g, h) for g in range(TILE // GROUP) for h in range(N_HEADS)]
    late_every = len(heads) // len(late_pieces)
    s_next = scores(*heads[0])
    o_heads = []
    for n, (g, h) in enumerate(heads):
        s = s_next
        if n + 1 < len(heads):
            s_next = scores(*heads[n + 1])
        if late_pieces and n % late_every == 0:
            late_pieces.pop(0)()
        o_heads.append(head_output(g, h, s))
        if h % 2 == 1:
            rows = slice(GROUP * g, GROUP * (g + 1))
            pair = slice(LANES * (h // 2), LANES * (h // 2 + 1))
            num = jnp.where(low_g, o_heads[0], o_heads[1])
            den = jnp.where(low_g, pltpu.roll(o_heads[0], HEAD_DIM, 1),
                            pltpu.roll(o_heads[1], HEAD_DIM, 1))
            o_s[rows, pair] = (num / den * ag_s[rows, pair]).astype(bf16)
            o_heads = []

    assert not late_pieces
    attn_out = jnp.dot(o_s[...], w_ao_ref[...], preferred_element_type=f32)
    a_all = jnp.concatenate([a_conv[c] for c in range(CONV_SLABS)], axis=1).astype(bf16)
    conv_out = jnp.dot(a_all, w_co_ref[...], preferred_element_type=f32)
    h = gc_s[...] * conv_out.astype(bf16) + ga_s[...] * attn_out.astype(bf16)
    y = jnp.dot(h, w_o_ref[...], preferred_element_type=f32) + b_o_ref[...]
    r = DEEPNORM_ALPHA * x_ref[...] + y
    mu = jnp.mean(r, axis=-1, keepdims=True)
    cen = r - mu
    var = jnp.mean(cen * cen, axis=-1, keepdims=True)
    out_ref[...] = cen * lax.rsqrt(var + LN_EPS) * oln_g_ref[...] + oln_b_ref[...]

    u_hist[:, 0:CONV_PAD, :] = u_hist[:, TILE:U_ROWS, :]
    kt_hist[:, 0:HIST] = kt_hist[:, TILE:TILE + HIST]
    v_hist[0:HIST, :] = v_hist[TILE:TILE + HIST, :]


def _rel_bias_table(rel_bias):
    n = GROUP + KWIN
    diag = np.arange(n)
    idx = np.clip(HIST + (GROUP - 1) - diag, -MAX_REL, MAX_REL) + MAX_REL
    n_far = int(np.sum(idx == 2 * MAX_REL)) - 1
    n_near = int(np.sum(idx == 0)) - 1
    rb = rel_bias.astype(jnp.float32)
    vec = jnp.concatenate([
        jnp.broadcast_to(rb[:, -1:], (N_HEADS, n_far)),
        rb[:, ::-1],
        jnp.broadcast_to(rb[:, :1], (N_HEADS, n_near))], axis=1)
    assert vec.shape == (N_HEADS, n)
    skew = jnp.tile(vec, (1, GROUP))[:, :GROUP * (n - 1)].reshape(N_HEADS, GROUP, n - 1)
    table = skew[:, :, GROUP - 1:GROUP - 1 + KWIN]
    r = np.arange(GROUP)[:, None]
    c = np.arange(KWIN)[None, :]
    band = c // CHUNK - r // CHUNK
    in_band = (band >= 0) & (band <= LEFT_CHUNKS)
    return jnp.where(jnp.asarray(in_band)[None], table * LOG2E, NEG)


def kernel(x, w_in, b_in, conv_w, conv_b, conv_ln_g, conv_ln_b, w_conv_out, rel_bias,
           w_attn_out, w_o, b_o, out_ln_g, out_ln_b):
    bsz, seq, d_model = x.shape
    assert d_model == D_MODEL and seq % TILE == 0 and w_in.shape == (D_MODEL, D_IN)
    f32, bf16 = jnp.float32, jnp.bfloat16

    def row(v):
        return v.astype(f32).reshape(1, -1)

    def full(shape):
        return pl.BlockSpec(shape, lambda b, s: (0,) * len(shape))

    operands = [
        (x, pl.BlockSpec((None, TILE, D_MODEL), lambda b, s: (b, s, 0))),
        (w_in.astype(bf16), full((D_MODEL, D_IN))),
        (row(b_in), full((1, D_IN))),
        (conv_w.astype(f32), full((CONV_WIDTH, D_CONV))),
        (row(conv_b), full((1, D_CONV))),
        (row(conv_ln_g), full((1, D_CONV))),
        (row(conv_ln_b), full((1, D_CONV))),
        (w_conv_out.astype(bf16), full((D_CONV, D_MODEL))),
        (_rel_bias_table(rel_bias), full((N_HEADS, GROUP, KWIN))),
        (w_attn_out.astype(bf16), full((D_ATTN, D_MODEL))),
        (w_o.astype(bf16), full((D_MODEL, D_MODEL))),
        (row(b_o), full((1, D_MODEL))),
        (row(out_ln_g), full((1, D_MODEL))),
        (row(out_ln_b), full((1, D_MODEL))),
    ]
    scratch = [
        pltpu.VMEM((TILE, D_MODEL), bf16),
        pltpu.VMEM((CONV_SLABS, U_ROWS, LANES), f32),
        pltpu.VMEM((CONV_SLABS, TILE, LANES), f32),
        pltpu.VMEM((CONV_SLABS, TILE, LANES), f32),
        pltpu.VMEM((TILE, D_MODEL), bf16),
        pltpu.VMEM((TILE, D_MODEL), bf16),
        pltpu.VMEM((TILE, D_ATTN), bf16),
        pltpu.VMEM((2 * D_ATTN, HIST + TILE), bf16),
        pltpu.VMEM((HIST + TILE, 2 * D_ATTN), bf16),
        pltpu.VMEM((TILE, D_ATTN), bf16),
        pltpu.VMEM((TILE, D_ATTN), bf16),
    ]
    return pl.pallas_call(
        _layer_kernel,
        grid=(bsz, seq // TILE),
        in_specs=[spec for _, spec in operands],
        out_specs=pl.BlockSpec((None, TILE, D_MODEL), lambda b, s: (b, s, 0)),
        out_shape=jax.ShapeDtypeStruct(x.shape, x.dtype),
        scratch_shapes=scratch,
        compiler_params=pltpu.CompilerParams(
            dimension_semantics=("arbitrary", "arbitrary"),
            vmem_limit_bytes=VMEM_LIMIT_BYTES),
        name="hybrid_layer",
    )(*[a for a, _ in operands])
```

```python
import functools
import math

import numpy as np
import jax
import jax.numpy as jnp
from jax import lax
from jax.experimental import pallas as pl
from jax.experimental.pallas import tpu as pltpu

D_MODEL = 1024
CHUNK = 64
LEFT_CHUNKS = 8
D_CONV = D_MODEL // 2
CONV_WIDTH = 31
N_HEADS = 8
HEAD_DIM = 64
D_ATTN = N_HEADS * HEAD_DIM
MAX_REL = 128
LN_EPS = 1e-5
DEPTH = 1
DEEPNORM_ALPHA = (2.0 * DEPTH) ** 0.25

C_VAL, C_GLU, C_GATE = 0, D_CONV, 2 * D_CONV
Q_OFF = 3 * D_CONV
K_OFF = Q_OFF + D_ATTN
V_OFF = K_OFF + D_ATTN
AG_OFF = V_OFF + D_ATTN
GC_OFF = AG_OFF + D_ATTN
GA_OFF = GC_OFF + D_MODEL
D_IN = GA_OFF + D_MODEL

LANES = 128
SUBLANES = 8
MXU_COLS = 256
TILE = 512
GROUP = 4 * CHUNK
HIST = LEFT_CHUNKS * CHUNK
KWIN = HIST + GROUP
CONV_PAD = 32
CONV_ROWS = 32
U_ROWS = CONV_PAD + TILE
CONV_SLABS = D_CONV // LANES
ROW_PHASES = 2
NEG = -1e30
LOG2E = math.log2(math.e)
VMEM_LIMIT_BYTES = 58 * 1024 * 1024

assert TILE == HIST and TILE % GROUP == 0 and CONV_PAD >= CONV_WIDTH - 1
assert CONV_PAD % SUBLANES == 0 and CONV_ROWS % (ROW_PHASES * SUBLANES) == 0


def _sigmoid(x):
    return jax.nn.sigmoid(x)


def _silu(x):
    return x * jax.nn.sigmoid(x)


def _layer_kernel(x_ref, w_in_ref, b_in_ref, conv_w_ref, conv_b_ref, cln_g_ref, cln_b_ref,
                  w_co_ref, bias_ref, w_ao_ref, w_o_ref, b_o_ref, oln_g_ref, oln_b_ref,
                  out_ref,
                  xb_s, u_hist, cg_s, a_conv, co_s, gc_s, ga_s, q_s, kt_hist, v_hist, ag_s, o_s):
    f32, bf16 = jnp.float32, jnp.bfloat16
    s_idx = pl.program_id(1)

    @pl.when(s_idx == 0)
    def _():
        u_hist[:, 0:CONV_PAD, :] = jnp.zeros((CONV_SLABS, CONV_PAD, LANES), f32)
        kt_hist[...] = jnp.zeros(kt_hist.shape, bf16)
        v_hist[0:HIST, :] = jnp.zeros((HIST, 2 * D_ATTN), bf16)

    xb_s[...] = x_ref[...].astype(bf16)

    def proj(lo, width):
        z = jnp.dot(xb_s[...], w_in_ref[:, lo:lo + width], preferred_element_type=f32)
        return z + b_in_ref[:, lo:lo + width]

    u = proj(C_VAL, D_CONV) * _sigmoid(proj(C_GLU, D_CONV))
    cg = _silu(proj(C_GATE, D_CONV))
    for c in range(CONV_SLABS):
        u_hist[c, CONV_PAD:U_ROWS, :] = u[:, LANES * c:LANES * (c + 1)]
        cg_s[c] = cg[:, LANES * c:LANES * (c + 1)]

    lane = lax.broadcasted_iota(jnp.int32, (TILE, LANES), 1)
    low = lane < HEAD_DIM

    def q_piece(c):
        q_s[:, c:c + MXU_COLS] = (
            proj(Q_OFF + c, MXU_COLS) * (LOG2E / math.sqrt(HEAD_DIM))).astype(bf16)

    def k_piece(c):
        kv = proj(K_OFF + c, MXU_COLS)
        for jj in range(MXU_COLS // LANES):
            pair_t = kv[:, LANES * jj:LANES * (jj + 1)].T.astype(bf16)
            h0 = 2 * (c // LANES + jj)
            kt_hist[LANES * h0:LANES * h0 + HEAD_DIM, HIST:HIST + TILE] = pair_t[0:HEAD_DIM]
            kt_hist[LANES * (h0 + 1) + HEAD_DIM:LANES * (h0 + 2), HIST:HIST + TILE] = (
                pair_t[HEAD_DIM:LANES])

    def v_piece(c):
        kv = proj(V_OFF + c, MXU_COLS)
        for jj in range(MXU_COLS // LANES):
            pair = kv[:, LANES * jj:LANES * (jj + 1)]
            base = 2 * (c + LANES * jj)
            v_hist[HIST:HIST + TILE, base:base + LANES] = jnp.where(low, pair, 1.0).astype(bf16)
            v_hist[HIST:HIST + TILE, base + LANES:base + 2 * LANES] = (
                jnp.where(low, 1.0, pair).astype(bf16))

    def gate_piece(off, dst, act, c):
        dst[:, c:c + MXU_COLS] = act(proj(off + c, MXU_COLS)).astype(bf16)

    pieces = []
    for c in range(0, D_ATTN, MXU_COLS):
        pieces.append(functools.partial(q_piece, c))
    for c in range(0, D_ATTN, MXU_COLS):
        pieces.append(functools.partial(k_piece, c))
    for c in range(0, D_ATTN, MXU_COLS):
        pieces.append(functools.partial(v_piece, c))
    for c in range(0, D_ATTN, MXU_COLS):
        pieces.append(functools.partial(gate_piece, AG_OFF, ag_s, _silu, c))
    late_pieces = []
    for c in range(0, D_MODEL, MXU_COLS):
        late_pieces.append(functools.partial(gate_piece, GC_OFF, gc_s, _sigmoid, c))
    for c in range(0, D_MODEL, MXU_COLS):
        late_pieces.append(functools.partial(gate_piece, GA_OFF, ga_s, _sigmoid, c))
    n_blocks = TILE // CONV_ROWS
    blocks_per_piece = n_blocks // len(pieces)
    assert blocks_per_piece * len(pieces) == n_blocks

    cln_g = cln_g_ref[...]
    cln_b = cln_b_ref[...]
    row0 = CONV_PAD - (CONV_WIDTH - 1)
    span = ROW_PHASES * SUBLANES

    def conv_block(rb):
        r0 = rb * CONV_ROWS
        sets = [(a, p) for a in range(CONV_ROWS // span) for p in range(ROW_PHASES)]

        def rows_of(a, p, shift=0):
            return pl.ds(shift + r0 + span * a + p, SUBLANES, stride=ROW_PHASES)

        acc = [[jnp.broadcast_to(conv_b_ref[:, LANES * c:LANES * (c + 1)], (SUBLANES, LANES))
                for c in range(CONV_SLABS)] for _ in sets]
        loaded = {}

        def taps(c, a, p, k):
            key = (c, a, p + k)
            if key not in loaded:
                loaded[key] = u_hist[c, rows_of(a, p, row0 + k), :]
            return loaded[key]

        for k in range(CONV_WIDTH):
            for c in range(CONV_SLABS):
                w_kc = conv_w_ref[k:k + 1, LANES * c:LANES * (c + 1)]
                for i, (a, p) in enumerate(sets):
                    acc[i][c] = acc[i][c] + w_kc * taps(c, a, p, k)
        conv = jnp.concatenate([jnp.concatenate(acc_i, axis=1) for acc_i in acc], axis=0)
        gate = jnp.concatenate(
            [jnp.concatenate([cg_s[c, rows_of(a, p), :] for c in range(CONV_SLABS)], axis=1)
             for a, p in sets], axis=0)
        mu = jnp.mean(conv, axis=-1, keepdims=True)
        cen = conv - mu
        var = jnp.mean(cen * cen, axis=-1, keepdims=True)
        y = cen * lax.rsqrt(var + LN_EPS) * cln_g + cln_b
        act = _silu(y) * gate
        for i, (a, p) in enumerate(sets):
            for c in range(CONV_SLABS):
                a_conv[c, rows_of(a, p), :] = act[SUBLANES * i:SUBLANES * (i + 1),
                                                  LANES * c:LANES * (c + 1)]

    for n, piece in enumerate(pieces):
        piece()
        for rb in range(blocks_per_piece * n, blocks_per_piece * (n + 1)):
            conv_block(rb)
    a_all = jnp.concatenate([a_conv[c] for c in range(CONV_SLABS)], axis=1).astype(bf16)
    co_s[...] = jnp.dot(a_all, w_co_ref[...], preferred_element_type=f32).astype(bf16)

    first = s_idx == 0
    off_first = jnp.where(first, jnp.float32(-NEG), jnp.float32(0.0))
    lane_g = lax.broadcasted_iota(jnp.int32, (GROUP, LANES), 1)
    low_g = lane_g < HEAD_DIM
    col_tiles = KWIN // LANES
    band_tiles = (LEFT_CHUNKS + 1) * CHUNK // LANES + 1
    zero_tile = jnp.zeros((CHUNK, LANES), bf16)

    def scores(g, h):
        qp = q_s[GROUP * g:GROUP * (g + 1), LANES * (h // 2):LANES * (h // 2 + 1)]
        kh_t = kt_hist[LANES * h:LANES * (h + 1), GROUP * g:GROUP * g + KWIN]
        return jnp.dot(qp, kh_t, preferred_element_type=f32)

    def head_output(g, h, s):
        n_pre = (HIST - GROUP * g) // GROUP
        offs = [off_first if t < n_pre else None for t in range(KWIN // GROUP)]
        e_rows = []
        for ci in range(GROUP // CHUNK):
            r = slice(CHUNK * ci, CHUNK * (ci + 1))
            first_tile = (CHUNK * ci) // LANES
            live = range(first_tile, first_tile + band_tiles)
            sc = {t: s[r, LANES * t:LANES * (t + 1)] + bias_ref[h, r, LANES * t:LANES * (t + 1)]
                  for t in live}
            tops = [sc[t] if offs[LANES * t // GROUP] is None
                    else sc[t] - offs[LANES * t // GROUP] for t in live]
            m = jnp.max(functools.reduce(jnp.maximum, tops), axis=-1, keepdims=True)
            e_tiles = []
            for t in range(col_tiles):
                if t in live:
                    off = offs[LANES * t // GROUP]
                    mt = m if off is None else m + off
                    e_tiles.append(jnp.exp2(sc[t] - mt).astype(bf16))
                else:
                    e_tiles.append(zero_tile)
            e_rows.append(jnp.concatenate(e_tiles, axis=1))
        e = jnp.concatenate(e_rows, axis=0)
        vh = v_hist[GROUP * g:GROUP * g + KWIN, LANES * h:LANES * (h + 1)]
        return jnp.dot(e, vh, preferred_element_type=f32)

    heads = [(g, h) for g in range(TILE // GROUP) for h in range(N_HEADS)]
    late_every = len(heads) // len(late_pieces)
    s_next = scores(*heads[0])
    o_heads = []
    for n, (g, h) in enumerate(heads):
        s = s_next
        if n + 1 < len(heads):
            s_next = scores(*heads[n + 1])
        if late_pieces and n % late_every == 0:
            late_pieces.pop(0)()
        o_heads.append(head_output(g, h, s))
        if h % 2 == 1:
            rows = slice(GROUP * g, GROUP * (g + 1))
            pair = slice(LANES * (h // 2), LANES * (h // 2 + 1))
            num = jnp.where(low_g, o_heads[0], o_heads[1])
            den = jnp.where(low_g, pltpu.roll(o_heads[0], HEAD_DIM, 1),
                            pltpu.roll(o_heads[1], HEAD_DIM, 1))
            o_s[rows, pair] = (num / den * ag_s[rows, pair]).astype(bf16)
            o_heads = []

    assert not late_pieces
    attn_out = jnp.dot(o_s[...], w_ao_ref[...], preferred_element_type=f32)
    h = gc_s[...] * co_s[...] + ga_s[...] * attn_out.astype(bf16)
    y = jnp.dot(h, w_o_ref[...], preferred_element_type=f32) + b_o_ref[...]
    r = DEEPNORM_ALPHA * x_ref[...] + y
    mu = jnp.mean(r, axis=-1, keepdims=True)
    cen = r - mu
    var = jnp.mean(cen * cen, axis=-1, keepdims=True)
    out_ref[...] = cen * lax.rsqrt(var + LN_EPS) * oln_g_ref[...] + oln_b_ref[...]

    u_hist[:, 0:CONV_PAD, :] = u_hist[:, TILE:U_ROWS, :]
    kt_hist[:, 0:HIST] = kt_hist[:, TILE:TILE + HIST]
    v_hist[0:HIST, :] = v_hist[TILE:TILE + HIST, :]


def _rel_bias_table(rel_bias):
    n = GROUP + KWIN
    diag = np.arange(n)
    idx = np.clip(HIST + (GROUP - 1) - diag, -MAX_REL, MAX_REL) + MAX_REL
    n_far = int(np.sum(idx == 2 * MAX_REL)) - 1
    n_near = int(np.sum(idx == 0)) - 1
    rb = rel_bias.astype(jnp.float32)
    vec = jnp.concatenate([
        jnp.broadcast_to(rb[:, -1:], (N_HEADS, n_far)),
        rb[:, ::-1],
        jnp.broadcast_to(rb[:, :1], (N_HEADS, n_near))], axis=1)
    assert vec.shape == (N_HEADS, n)
    skew = jnp.tile(vec, (1, GROUP))[:, :GROUP * (n - 1)].reshape(N_HEADS, GROUP, n - 1)
    table = skew[:, :, GROUP - 1:GROUP - 1 + KWIN]
    r = np.arange(GROUP)[:, None]
    c = np.arange(KWIN)[None, :]
    band = c // CHUNK - r // CHUNK
    in_band = (band >= 0) & (band <= LEFT_CHUNKS)
    return jnp.where(jnp.asarray(in_band)[None], table * LOG2E, NEG)


def kernel(x, w_in, b_in, conv_w, conv_b, conv_ln_g, conv_ln_b, w_conv_out, rel_bias,
           w_attn_out, w_o, b_o, out_ln_g, out_ln_b):
    bsz, seq, d_model = x.shape
    assert d_model == D_MODEL and seq % TILE == 0 and w_in.shape == (D_MODEL, D_IN)
    f32, bf16 = jnp.float32, jnp.bfloat16

    def row(v):
        return v.astype(f32).reshape(1, -1)

    def full(shape):
        return pl.BlockSpec(shape, lambda b, s: (0,) * len(shape))

    operands = [
        (x, pl.BlockSpec((None, TILE, D_MODEL), lambda b, s: (b, s, 0))),
        (w_in.astype(bf16), full((D_MODEL, D_IN))),
        (row(b_in), full((1, D_IN))),
        (conv_w.astype(f32), full((CONV_WIDTH, D_CONV))),
        (row(conv_b), full((1, D_CONV))),
        (row(conv_ln_g), full((1, D_CONV))),
        (row(conv_ln_b), full((1, D_CONV))),
        (w_conv_out.astype(bf16), full((D_CONV, D_MODEL))),
        (_rel_bias_table(rel_bias), full((N_HEADS, GROUP, KWIN))),
        (w_attn_out.astype(bf16), full((D_ATTN, D_MODEL))),
        (w_o.astype(bf16), full((D_MODEL, D_MODEL))),
        (row(b_o), full((1, D_MODEL))),
        (row(out_ln_g), full((1, D_MODEL))),
        (row(out_ln_b), full((1, D_MODEL))),
    ]
    scratch = [
        pltpu.VMEM((TILE, D_MODEL), bf16),
        pltpu.VMEM((CONV_SLABS, U_ROWS, LANES), f32),
        pltpu.VMEM((CONV_SLABS, TILE, LANES), f32),
        pltpu.VMEM((CONV_SLABS, TILE, LANES), f32),
        pltpu.VMEM((TILE, D_MODEL), bf16),
        pltpu.VMEM((TILE, D_MODEL), bf16),
        pltpu.VMEM((TILE, D_MODEL), bf16),
        pltpu.VMEM((TILE, D_ATTN), bf16),
        pltpu.VMEM((2 * D_ATTN, HIST + TILE), bf16),
        pltpu.VMEM((HIST + TILE, 2 * D_ATTN), bf16),
        pltpu.VMEM((TILE, D_ATTN), bf16),
        pltpu.VMEM((TILE, D_ATTN), bf16),
    ]
    return pl.pallas_call(
        _layer_kernel,
        grid=(bsz, seq // TILE),
        in_specs=[spec for _, spec in operands],
        out_specs=pl.BlockSpec((None, TILE, D_MODEL), lambda b, s: (b, s, 0)),
        out_shape=jax.ShapeDtypeStruct(x.shape, x.dtype),
        scratch_shapes=scratch,
        compiler_params=pltpu.CompilerParams(
            dimension_semantics=("arbitrary", "arbitrary"),
            vmem_limit_bytes=VMEM_LIMIT_BYTES),
        name="hybrid_layer",
    )(*[a for a, _ in operands])
```

```python
import functools
import math

import numpy as np
import jax
import jax.numpy as jnp
from jax import lax
from jax.experimental import pallas as pl
from jax.experimental.pallas import tpu as pltpu

D_MODEL = 1024
CHUNK = 64
LEFT_CHUNKS = 8
D_CONV = D_MODEL // 2
CONV_WIDTH = 31
N_HEADS = 8
HEAD_DIM = 64
D_ATTN = N_HEADS * HEAD_DIM
MAX_REL = 128
LN_EPS = 1e-5
DEPTH = 1
DEEPNORM_ALPHA = (2.0 * DEPTH) ** 0.25

C_VAL, C_GLU, C_GATE = 0, D_CONV, 2 * D_CONV
Q_OFF = 3 * D_CONV
K_OFF = Q_OFF + D_ATTN
V_OFF = K_OFF + D_ATTN
AG_OFF = V_OFF + D_ATTN
GC_OFF = AG_OFF + D_ATTN
GA_OFF = GC_OFF + D_MODEL
D_IN = GA_OFF + D_MODEL

LANES = 128
SUBLANES = 8
MXU_COLS = 256
TILE = 512
GROUP = 4 * CHUNK
HIST = LEFT_CHUNKS * CHUNK
KWIN = HIST + GROUP
CONV_PAD = 32
CONV_ROWS = 32
SCORES_AHEAD = 2
U_ROWS = CONV_PAD + TILE
CONV_SLABS = D_CONV // LANES
ROW_PHASES = 2
NEG = -1e30
LOG2E = math.log2(math.e)
VMEM_LIMIT_BYTES = 58 * 1024 * 1024

assert TILE == HIST and TILE % GROUP == 0 and CONV_PAD >= CONV_WIDTH - 1
assert CONV_PAD % SUBLANES == 0 and CONV_ROWS % (ROW_PHASES * SUBLANES) == 0


def _sigmoid(x):
    return jax.nn.sigmoid(x)


def _silu(x):
    return x * jax.nn.sigmoid(x)


def _layer_kernel(x_ref, w_in_ref, b_in_ref, conv_w_ref, conv_b_ref, cln_g_ref, cln_b_ref,
                  w_co_ref, bias_ref, w_ao_ref, w_o_ref, b_o_ref, oln_g_ref, oln_b_ref,
                  out_ref,
                  xb_s, u_hist, cg_s, a_conv, gc_s, ga_s, q_s, kt_hist, v_hist, ag_s, o_s):
    f32, bf16 = jnp.float32, jnp.bfloat16
    s_idx = pl.program_id(1)

    @pl.when(s_idx == 0)
    def _():
        u_hist[:, 0:CONV_PAD, :] = jnp.zeros((CONV_SLABS, CONV_PAD, LANES), f32)
        kt_hist[...] = jnp.zeros(kt_hist.shape, bf16)
        v_hist[0:HIST, :] = jnp.zeros((HIST, 2 * D_ATTN), bf16)

    xb_s[...] = x_ref[...].astype(bf16)

    def proj(lo, width):
        z = jnp.dot(xb_s[...], w_in_ref[:, lo:lo + width], preferred_element_type=f32)
        return z + b_in_ref[:, lo:lo + width]

    u = proj(C_VAL, D_CONV) * _sigmoid(proj(C_GLU, D_CONV))
    cg = _silu(proj(C_GATE, D_CONV))
    for c in range(CONV_SLABS):
        u_hist[c, CONV_PAD:U_ROWS, :] = u[:, LANES * c:LANES * (c + 1)]
        cg_s[c] = cg[:, LANES * c:LANES * (c + 1)]

    lane = lax.broadcasted_iota(jnp.int32, (TILE, LANES), 1)
    low = lane < HEAD_DIM

    def q_piece(c):
        q_s[:, c:c + MXU_COLS] = (
            proj(Q_OFF + c, MXU_COLS) * (LOG2E / math.sqrt(HEAD_DIM))).astype(bf16)

    def k_piece(c):
        kv = proj(K_OFF + c, MXU_COLS)
        for jj in range(MXU_COLS // LANES):
            pair_t = kv[:, LANES * jj:LANES * (jj + 1)].T.astype(bf16)
            h0 = 2 * (c // LANES + jj)
            kt_hist[LANES * h0:LANES * h0 + HEAD_DIM, HIST:HIST + TILE] = pair_t[0:HEAD_DIM]
            kt_hist[LANES * (h0 + 1) + HEAD_DIM:LANES * (h0 + 2), HIST:HIST + TILE] = (
                pair_t[HEAD_DIM:LANES])

    def v_piece(c):
        kv = proj(V_OFF + c, MXU_COLS)
        for jj in range(MXU_COLS // LANES):
            pair = kv[:, LANES * jj:LANES * (jj + 1)]
            base = 2 * (c + LANES * jj)
            v_hist[HIST:HIST + TILE, base:base + LANES] = jnp.where(low, pair, 1.0).astype(bf16)
            v_hist[HIST:HIST + TILE, base + LANES:base + 2 * LANES] = (
                jnp.where(low, 1.0, pair).astype(bf16))

    def gate_piece(off, dst, act, c):
        dst[:, c:c + MXU_COLS] = act(proj(off + c, MXU_COLS)).astype(bf16)

    pieces = []
    for c in range(0, D_ATTN, MXU_COLS):
        pieces.append(functools.partial(q_piece, c))
    for c in range(0, D_ATTN, MXU_COLS):
        pieces.append(functools.partial(k_piece, c))
    for c in range(0, D_ATTN, MXU_COLS):
        pieces.append(functools.partial(v_piece, c))
    for c in range(0, D_ATTN, MXU_COLS):
        pieces.append(functools.partial(gate_piece, AG_OFF, ag_s, _silu, c))
    late_pieces = []
    for c in range(0, D_MODEL, MXU_COLS):
        late_pieces.append(functools.partial(gate_piece, GC_OFF, gc_s, _sigmoid, c))
    for c in range(0, D_MODEL, MXU_COLS):
        late_pieces.append(functools.partial(gate_piece, GA_OFF, ga_s, _sigmoid, c))
    n_blocks = TILE // CONV_ROWS
    blocks_per_piece = n_blocks // len(pieces)
    assert blocks_per_piece * len(pieces) == n_blocks

    cln_g = cln_g_ref[...]
    cln_b = cln_b_ref[...]
    row0 = CONV_PAD - (CONV_WIDTH - 1)
    span = ROW_PHASES * SUBLANES

    def conv_block(rb):
        r0 = rb * CONV_ROWS
        sets = [(a, p) for a in range(CONV_ROWS // span) for p in range(ROW_PHASES)]

        def rows_of(a, p, shift=0):
            return pl.ds(shift + r0 + span * a + p, SUBLANES, stride=ROW_PHASES)

        acc = [[jnp.broadcast_to(conv_b_ref[:, LANES * c:LANES * (c + 1)], (SUBLANES, LANES))
                for c in range(CONV_SLABS)] for _ in sets]
        loaded = {}

        def taps(c, a, p, k):
            key = (c, a, p + k)
            if key not in loaded:
                loaded[key] = u_hist[c, rows_of(a, p, row0 + k), :]
            return loaded[key]

        for k in range(CONV_WIDTH):
            for c in range(CONV_SLABS):
                w_kc = conv_w_ref[k:k + 1, LANES * c:LANES * (c + 1)]
                for i, (a, p) in enumerate(sets):
                    acc[i][c] = acc[i][c] + w_kc * taps(c, a, p, k)
        conv = jnp.concatenate([jnp.concatenate(acc_i, axis=1) for acc_i in acc], axis=0)
        gate = jnp.concatenate(
            [jnp.concatenate([cg_s[c, rows_of(a, p), :] for c in range(CONV_SLABS)], axis=1)
             for a, p in sets], axis=0)
        mu = jnp.mean(conv, axis=-1, keepdims=True)
        cen = conv - mu
        var = jnp.mean(cen * cen, axis=-1, keepdims=True)
        y = cen * lax.rsqrt(var + LN_EPS) * cln_g + cln_b
        act = _silu(y) * gate
        for i, (a, p) in enumerate(sets):
            for c in range(CONV_SLABS):
                a_conv[c, rows_of(a, p), :] = act[SUBLANES * i:SUBLANES * (i + 1),
                                                  LANES * c:LANES * (c + 1)]

    for n, piece in enumerate(pieces):
        piece()
        for rb in range(blocks_per_piece * n, blocks_per_piece * (n + 1)):
            conv_block(rb)

    first = s_idx == 0
    off_first = jnp.where(first, jnp.float32(-NEG), jnp.float32(0.0))
    lane_g = lax.broadcasted_iota(jnp.int32, (GROUP, LANES), 1)
    low_g = lane_g < HEAD_DIM
    col_tiles = KWIN // LANES
    band_tiles = (LEFT_CHUNKS + 1) * CHUNK // LANES + 1
    zero_tile = jnp.zeros((CHUNK, LANES), bf16)

    def scores(g, h):
        qp = q_s[GROUP * g:GROUP * (g + 1), LANES * (h // 2):LANES * (h // 2 + 1)]
        kh_t = kt_hist[LANES * h:LANES * (h + 1), GROUP * g:GROUP * g + KWIN]
        return jnp.dot(qp, kh_t, preferred_element_type=f32)

    def head_output(g, h, s):
        n_pre = (HIST - GROUP * g) // GROUP
        offs = [off_first if t < n_pre else None for t in range(KWIN // GROUP)]
        e_rows = []
        for ci in range(GROUP // CHUNK):
            r = slice(CHUNK * ci, CHUNK * (ci + 1))
            first_tile = (CHUNK * ci) // LANES
            live = range(first_tile, first_tile + band_tiles)
            sc = {t: s[r, LANES * t:LANES * (t + 1)] + bias_ref[h, r, LANES * t:LANES * (t + 1)]
                  for t in live}
            tops = [sc[t] if offs[LANES * t // GROUP] is None
                    else sc[t] - offs[LANES * t // GROUP] for t in live]
            m = jnp.max(functools.reduce(jnp.maximum, tops), axis=-1, keepdims=True)
            e_tiles = []
            for t in range(col_tiles):
                if t in live:
                    off = offs[LANES * t // GROUP]
                    mt = m if off is None else m + off
                    e_tiles.append(jnp.exp2(sc[t] - mt).astype(bf16))
                else:
                    e_tiles.append(zero_tile)
            e_rows.append(jnp.concatenate(e_tiles, axis=1))
        e = jnp.concatenate(e_rows, axis=0)
        vh = v_hist[GROUP * g:GROUP * g + KWIN, LANES * h:LANES * (h + 1)]
        return jnp.dot(e, vh, preferred_element_type=f32)

    heads = [(g, h) for g in range(TILE // GROUP) for h in range(N_HEADS)]
    late_every = len(heads) // len(late_pieces)
    pending = [scores(*head) for head in heads[:SCORES_AHEAD]]
    o_heads = []
    for n, (g, h) in enumerate(heads):
        s = pending.pop(0)
        if n + SCORES_AHEAD < len(heads):
            pending.append(scores(*heads[n + SCORES_AHEAD]))
        if late_pieces and n % late_every == 0:
            late_pieces.pop(0)()
        o_heads.append(head_output(g, h, s))
        if h % 2 == 1:
            rows = slice(GROUP * g, GROUP * (g + 1))
            pair = slice(LANES * (h // 2), LANES * (h // 2 + 1))
            num = jnp.where(low_g, o_heads[0], o_heads[1])
            den = jnp.where(low_g, pltpu.roll(o_heads[0], HEAD_DIM, 1),
                            pltpu.roll(o_heads[1], HEAD_DIM, 1))
            o_s[rows, pair] = (num / den * ag_s[rows, pair]).astype(bf16)
            o_heads = []

    assert not late_pieces
    a_all = jnp.concatenate([a_conv[c] for c in range(CONV_SLABS)], axis=1).astype(bf16)
    conv_out = jnp.dot(a_all, w_co_ref[...], preferred_element_type=f32)
    attn_out = jnp.dot(o_s[...], w_ao_ref[...], preferred_element_type=f32)
    h = gc_s[...] * conv_out.astype(bf16) + ga_s[...] * attn_out.astype(bf16)
    y = jnp.dot(h, w_o_ref[...], preferred_element_type=f32) + b_o_ref[...]
    r = DEEPNORM_ALPHA * x_ref[...] + y
    mu = jnp.mean(r, axis=-1, keepdims=True)
    cen = r - mu
    var = jnp.mean(cen * cen, axis=-1, keepdims=True)
    out_ref[...] = cen * lax.rsqrt(var + LN_EPS) * oln_g_ref[...] + oln_b_ref[...]

    u_hist[:, 0:CONV_PAD, :] = u_hist[:, TILE:U_ROWS, :]
    kt_hist[:, 0:HIST] = kt_hist[:, TILE:TILE + HIST]
    v_hist[0:HIST, :] = v_hist[TILE:TILE + HIST, :]


def _rel_bias_table(rel_bias):
    n = GROUP + KWIN
    diag = np.arange(n)
    idx = np.clip(HIST + (GROUP - 1) - diag, -MAX_REL, MAX_REL) + MAX_REL
    n_far = int(np.sum(idx == 2 * MAX_REL)) - 1
    n_near = int(np.sum(idx == 0)) - 1
    rb = rel_bias.astype(jnp.float32)
    vec = jnp.concatenate([
        jnp.broadcast_to(rb[:, -1:], (N_HEADS, n_far)),
        rb[:, ::-1],
        jnp.broadcast_to(rb[:, :1], (N_HEADS, n_near))], axis=1)
    assert vec.shape == (N_HEADS, n)
    skew = jnp.tile(vec, (1, GROUP))[:, :GROUP * (n - 1)].reshape(N_HEADS, GROUP, n - 1)
    table = skew[:, :, GROUP - 1:GROUP - 1 + KWIN]
    r = np.arange(GROUP)[:, None]
    c = np.arange(KWIN)[None, :]
    band = c // CHUNK - r // CHUNK
    in_band = (band >= 0) & (band <= LEFT_CHUNKS)
    return jnp.where(jnp.asarray(in_band)[None], table * LOG2E, NEG)


def kernel(x, w_in, b_in, conv_w, conv_b, conv_ln_g, conv_ln_b, w_conv_out, rel_bias,
           w_attn_out, w_o, b_o, out_ln_g, out_ln_b):
    bsz, seq, d_model = x.shape
    assert d_model == D_MODEL and seq % TILE == 0 and w_in.shape == (D_MODEL, D_IN)
    f32, bf16 = jnp.float32, jnp.bfloat16

    def row(v):
        return v.astype(f32).reshape(1, -1)

    def full(shape):
        return pl.BlockSpec(shape, lambda b, s: (0,) * len(shape))

    operands = [
        (x, pl.BlockSpec((None, TILE, D_MODEL), lambda b, s: (b, s, 0))),
        (w_in.astype(bf16), full((D_MODEL, D_IN))),
        (row(b_in), full((1, D_IN))),
        (conv_w.astype(f32), full((CONV_WIDTH, D_CONV))),
        (row(conv_b), full((1, D_CONV))),
        (row(conv_ln_g), full((1, D_CONV))),
        (row(conv_ln_b), full((1, D_CONV))),
        (w_conv_out.astype(bf16), full((D_CONV, D_MODEL))),
        (_rel_bias_table(rel_bias), full((N_HEADS, GROUP, KWIN))),
        (w_attn_out.astype(bf16), full((D_ATTN, D_MODEL))),
        (w_o.astype(bf16), full((D_MODEL, D_MODEL))),
        (row(b_o), full((1, D_MODEL))),
        (row(out_ln_g), full((1, D_MODEL))),
        (row(out_ln_b), full((1, D_MODEL))),
    ]
    scratch = [
        pltpu.VMEM((TILE, D_MODEL), bf16),
        pltpu.VMEM((CONV_SLABS, U_ROWS, LANES), f32),
        pltpu.VMEM((CONV_SLABS, TILE, LANES), f32),
        pltpu.VMEM((CONV_SLABS, TILE, LANES), f32),
        pltpu.VMEM((TILE, D_MODEL), bf16),
        pltpu.VMEM((TILE, D_MODEL), bf16),
        pltpu.VMEM((TILE, D_ATTN), bf16),
        pltpu.VMEM((2 * D_ATTN, HIST + TILE), bf16),
        pltpu.VMEM((HIST + TILE, 2 * D_ATTN), bf16),
        pltpu.VMEM((TILE, D_ATTN), bf16),
        pltpu.VMEM((TILE, D_ATTN), bf16),
    ]
    return pl.pallas_call(
        _layer_kernel,
        grid=(bsz, seq // TILE),
        in_specs=[spec for _, spec in operands],
        out_specs=pl.BlockSpec((None, TILE, D_MODEL), lambda b, s: (b, s, 0)),
        out_shape=jax.ShapeDtypeStruct(x.shape, x.dtype),
        scratch_shapes=scratch,
        compiler_params=pltpu.CompilerParams(
            dimension_semantics=("arbitrary", "arbitrary"),
            vmem_limit_bytes=VMEM_LIMIT_BYTES),
        name="hybrid_layer",
    )(*[a for a, _ in operands])
```

```python
import functools
import math

import numpy as np
import jax
import jax.numpy as jnp
from jax import lax
from jax.experimental import pallas as pl
from jax.experimental.pallas import tpu as pltpu

D_MODEL = 1024
CHUNK = 64
LEFT_CHUNKS = 8
D_CONV = D_MODEL // 2
CONV_WIDTH = 31
N_HEADS = 8
HEAD_DIM = 64
D_ATTN = N_HEADS * HEAD_DIM
MAX_REL = 128
LN_EPS = 1e-5
DEPTH = 1
DEEPNORM_ALPHA = (2.0 * DEPTH) ** 0.25

C_VAL, C_GLU, C_GATE = 0, D_CONV, 2 * D_CONV
Q_OFF = 3 * D_CONV
K_OFF = Q_OFF + D_ATTN
V_OFF = K_OFF + D_ATTN
AG_OFF = V_OFF + D_ATTN
GC_OFF = AG_OFF + D_ATTN
GA_OFF = GC_OFF + D_MODEL
D_IN = GA_OFF + D_MODEL

LANES = 128
SUBLANES = 8
MXU_COLS = 256
TILE = 512
GROUP = 4 * CHUNK
HIST = LEFT_CHUNKS * CHUNK
KWIN = HIST + GROUP
CONV_PAD = 32
CONV_ROWS = 32
SCORES_AHEAD = 1
U_ROWS = CONV_PAD + TILE
CONV_SLABS = D_CONV // LANES
ROW_PHASES = 2
NEG = -1e30
LOG2E = math.log2(math.e)
VMEM_LIMIT_BYTES = 58 * 1024 * 1024

assert TILE == HIST and TILE % GROUP == 0 and CONV_PAD >= CONV_WIDTH - 1
assert CONV_PAD % SUBLANES == 0 and CONV_ROWS % (ROW_PHASES * SUBLANES) == 0


def _sigmoid(x):
    return jax.nn.sigmoid(x)


def _silu(x):
    return x * jax.nn.sigmoid(x)


def _layer_kernel(x_ref, w_in_ref, b_in_ref, conv_w_ref, conv_b_ref, cln_g_ref, cln_b_ref,
                  w_co_ref, bias_ref, w_ao_ref, w_o_ref, b_o_ref, oln_g_ref, oln_b_ref,
                  out_ref,
                  xb_s, u_hist, cg_s, a_conv, gc_s, ga_s, q_s, kt_hist, v_hist, ag_s, o_s):
    f32, bf16 = jnp.float32, jnp.bfloat16
    s_idx = pl.program_id(1)

    @pl.when(s_idx == 0)
    def _():
        u_hist[:, 0:CONV_PAD, :] = jnp.zeros((CONV_SLABS, CONV_PAD, LANES), f32)
        kt_hist[...] = jnp.zeros(kt_hist.shape, bf16)
        v_hist[0:HIST, :] = jnp.zeros((HIST, 2 * D_ATTN), bf16)

    xb_s[...] = x_ref[...].astype(bf16)

    def proj(lo, width):
        z = jnp.dot(xb_s[...], w_in_ref[:, lo:lo + width], preferred_element_type=f32)
        return z + b_in_ref[:, lo:lo + width]

    u = proj(C_VAL, D_CONV) * _sigmoid(proj(C_GLU, D_CONV))
    cg = _silu(proj(C_GATE, D_CONV))
    for c in range(CONV_SLABS):
        u_hist[c, CONV_PAD:U_ROWS, :] = u[:, LANES * c:LANES * (c + 1)]
        cg_s[c] = cg[:, LANES * c:LANES * (c + 1)]

    lane = lax.broadcasted_iota(jnp.int32, (TILE, LANES), 1)
    low = lane < HEAD_DIM

    def q_piece(c):
        q_s[:, c:c + MXU_COLS] = (
            proj(Q_OFF + c, MXU_COLS) * (LOG2E / math.sqrt(HEAD_DIM))).astype(bf16)

    def k_piece(c):
        kv = proj(K_OFF + c, MXU_COLS)
        for jj in range(MXU_COLS // LANES):
            pair_t = kv[:, LANES * jj:LANES * (jj + 1)].T.astype(bf16)
            h0 = 2 * (c // LANES + jj)
            kt_hist[LANES * h0:LANES * h0 + HEAD_DIM, HIST:HIST + TILE] = pair_t[0:HEAD_DIM]
            kt_hist[LANES * (h0 + 1) + HEAD_DIM:LANES * (h0 + 2), HIST:HIST + TILE] = (
                pair_t[HEAD_DIM:LANES])

    def v_piece(c):
        kv = proj(V_OFF + c, MXU_COLS)
        for jj in range(MXU_COLS // LANES):
            pair = kv[:, LANES * jj:LANES * (jj + 1)]
            base = 2 * (c + LANES * jj)
            v_hist[HIST:HIST + TILE, base:base + LANES] = jnp.where(low, pair, 1.0).astype(bf16)
            v_hist[HIST:HIST + TILE, base + LANES:base + 2 * LANES] = (
                jnp.where(low, 1.0, pair).astype(bf16))

    def gate_piece(off, dst, act, c):
        dst[:, c:c + MXU_COLS] = act(proj(off + c, MXU_COLS)).astype(bf16)

    pieces = []
    for c in range(0, D_ATTN, MXU_COLS):
        pieces.append(functools.partial(q_piece, c))
    for c in range(0, D_ATTN, MXU_COLS):
        pieces.append(functools.partial(k_piece, c))
    for c in range(0, D_ATTN, MXU_COLS):
        pieces.append(functools.partial(v_piece, c))
    for c in range(0, D_ATTN, MXU_COLS):
        pieces.append(functools.partial(gate_piece, AG_OFF, ag_s, _silu, c))
    late_pieces = []
    for c in range(0, D_MODEL, MXU_COLS):
        late_pieces.append(functools.partial(gate_piece, GC_OFF, gc_s, _sigmoid, c))
    for c in range(0, D_MODEL, MXU_COLS):
        late_pieces.append(functools.partial(gate_piece, GA_OFF, ga_s, _sigmoid, c))
    n_blocks = TILE // CONV_ROWS
    blocks_per_piece = n_blocks // len(pieces)
    assert blocks_per_piece * len(pieces) == n_blocks

    cln_g = cln_g_ref[...]
    cln_b = cln_b_ref[...]
    row0 = CONV_PAD - (CONV_WIDTH - 1)
    span = ROW_PHASES * SUBLANES

    def conv_block(rb):
        r0 = rb * CONV_ROWS
        sets = [(a, p) for a in range(CONV_ROWS // span) for p in range(ROW_PHASES)]

        def rows_of(a, p, shift=0):
            return pl.ds(shift + r0 + span * a + p, SUBLANES, stride=ROW_PHASES)

        acc = [[jnp.broadcast_to(conv_b_ref[:, LANES * c:LANES * (c + 1)], (SUBLANES, LANES))
                for c in range(CONV_SLABS)] for _ in sets]
        loaded = {}

        def taps(c, a, p, k):
            key = (c, a, p + k)
            if key not in loaded:
                loaded[key] = u_hist[c, rows_of(a, p, row0 + k), :]
            return loaded[key]

        for k in range(CONV_WIDTH):
            for c in range(CONV_SLABS):
                w_kc = conv_w_ref[k:k + 1, LANES * c:LANES * (c + 1)]
                for i, (a, p) in enumerate(sets):
                    acc[i][c] = acc[i][c] + w_kc * taps(c, a, p, k)
        conv = jnp.concatenate([jnp.concatenate(acc_i, axis=1) for acc_i in acc], axis=0)
        gate = jnp.concatenate(
            [jnp.concatenate([cg_s[c, rows_of(a, p), :] for c in range(CONV_SLABS)], axis=1)
             for a, p in sets], axis=0)
        mu = jnp.mean(conv, axis=-1, keepdims=True)
        cen = conv - mu
        var = jnp.mean(cen * cen, axis=-1, keepdims=True)
        y = cen * lax.rsqrt(var + LN_EPS) * cln_g + cln_b
        act = _silu(y) * gate
        for i, (a, p) in enumerate(sets):
            for c in range(CONV_SLABS):
                a_conv[c, rows_of(a, p), :] = act[SUBLANES * i:SUBLANES * (i + 1),
                                                  LANES * c:LANES * (c + 1)]

    for n, piece in enumerate(pieces):
        piece()
        for rb in range(blocks_per_piece * n, blocks_per_piece * (n + 1)):
            conv_block(rb)

    first = s_idx == 0
    off_first = jnp.where(first, jnp.float32(-NEG), jnp.float32(0.0))
    lane_g = lax.broadcasted_iota(jnp.int32, (GROUP, LANES), 1)
    low_g = lane_g < HEAD_DIM
    col_tiles = KWIN // LANES
    band_tiles = (LEFT_CHUNKS + 1) * CHUNK // LANES + 1
    zero_tile = jnp.zeros((CHUNK, LANES), bf16)

    def scores(g, h):
        qp = q_s[GROUP * g:GROUP * (g + 1), LANES * (h // 2):LANES * (h // 2 + 1)]
        kh_t = kt_hist[LANES * h:LANES * (h + 1), GROUP * g:GROUP * g + KWIN]
        return jnp.dot(qp, kh_t, preferred_element_type=f32)

    def head_output(g, h, s):
        n_pre = (HIST - GROUP * g) // GROUP
        offs = [off_first if t < n_pre else None for t in range(KWIN // GROUP)]
        e_rows = []
        for ci in range(GROUP // CHUNK):
            r = slice(CHUNK * ci, CHUNK * (ci + 1))
            first_tile = (CHUNK * ci) // LANES
            live = range(first_tile, first_tile + band_tiles)
            sc = {t: s[r, LANES * t:LANES * (t + 1)] + bias_ref[h, r, LANES * t:LANES * (t + 1)]
                  for t in live}
            tops = [sc[t] if offs[LANES * t // GROUP] is None
                    else sc[t] - offs[LANES * t // GROUP] for t in live]
            m = jnp.max(functools.reduce(jnp.maximum, tops), axis=-1, keepdims=True)
            e_tiles = []
            for t in range(col_tiles):
                if t in live:
                    off = offs[LANES * t // GROUP]
                    mt = m if off is None else m + off
                    e_tiles.append(jnp.exp2(sc[t] - mt).astype(bf16))
                else:
                    e_tiles.append(zero_tile)
            e_rows.append(jnp.concatenate(e_tiles, axis=1))
        e = jnp.concatenate(e_rows, axis=0)
        vh = v_hist[GROUP * g:GROUP * g + KWIN, LANES * h:LANES * (h + 1)]
        return jnp.dot(e, vh, preferred_element_type=f32)

    heads = [(g, h) for g in range(TILE // GROUP) for h in range(N_HEADS)]
    late_every = len(heads) // len(late_pieces)
    pending = [scores(*head) for head in heads[:SCORES_AHEAD]]
    o_heads = []
    for n, (g, h) in enumerate(heads):
        s = pending.pop(0)
        if n + SCORES_AHEAD < len(heads):
            pending.append(scores(*heads[n + SCORES_AHEAD]))
        if late_pieces and n % late_every == 0:
            late_pieces.pop(0)()
        o_heads.append(head_output(g, h, s))
        if h % 2 == 1:
            rows = slice(GROUP * g, GROUP * (g + 1))
            pair = slice(LANES * (h // 2), LANES * (h // 2 + 1))
            num = jnp.where(low_g, o_heads[0], o_heads[1])
            den = jnp.where(low_g, pltpu.roll(o_heads[0], HEAD_DIM, 1),
                            pltpu.roll(o_heads[1], HEAD_DIM, 1))
            o_s[rows, pair] = (num / den * ag_s[rows, pair]).astype(bf16)
            o_heads = []

    assert not late_pieces
    attn_out = jnp.dot(o_s[...], w_ao_ref[...], preferred_element_type=f32)
    a_all = jnp.concatenate([a_conv[c] for c in range(CONV_SLABS)], axis=1).astype(bf16)
    conv_out = jnp.dot(a_all, w_co_ref[...], preferred_element_type=f32)
    for rows in (slice(0, TILE // 2), slice(TILE // 2, TILE)):
        h = (gc_s[rows, :] * conv_out[rows, :].astype(bf16)
             + ga_s[rows, :] * attn_out[rows, :].astype(bf16))
        y = jnp.dot(h, w_o_ref[...], preferred_element_type=f32) + b_o_ref[...]
        r = DEEPNORM_ALPHA * x_ref[rows, :] + y
        mu = jnp.mean(r, axis=-1, keepdims=True)
        cen = r - mu
        var = jnp.mean(cen * cen, axis=-1, keepdims=True)
        out_ref[rows, :] = cen * lax.rsqrt(var + LN_EPS) * oln_g_ref[...] + oln_b_ref[...]

    u_hist[:, 0:CONV_PAD, :] = u_hist[:, TILE:U_ROWS, :]
    kt_hist[:, 0:HIST] = kt_hist[:, TILE:TILE + HIST]
    v_hist[0:HIST, :] = v_hist[TILE:TILE + HIST, :]


def _rel_bias_table(rel_bias):
    n = GROUP + KWIN
    diag = np.arange(n)
    idx = np.clip(HIST + (GROUP - 1) - diag, -MAX_REL, MAX_REL) + MAX_REL
    n_far = int(np.sum(idx == 2 * MAX_REL)) - 1
    n_near = int(np.sum(idx == 0)) - 1
    rb = rel_bias.astype(jnp.float32)
    vec = jnp.concatenate([
        jnp.broadcast_to(rb[:, -1:], (N_HEADS, n_far)),
        rb[:, ::-1],
        jnp.broadcast_to(rb[:, :1], (N_HEADS, n_near))], axis=1)
    assert vec.shape == (N_HEADS, n)
    skew = jnp.tile(vec, (1, GROUP))[:, :GROUP * (n - 1)].reshape(N_HEADS, GROUP, n - 1)
    table = skew[:, :, GROUP - 1:GROUP - 1 + KWIN]
    r = np.arange(GROUP)[:, None]
    c = np.arange(KWIN)[None, :]
    band = c // CHUNK - r // CHUNK
    in_band = (band >= 0) & (band <= LEFT_CHUNKS)
    return jnp.where(jnp.asarray(in_band)[None], table * LOG2E, NEG)


def kernel(x, w_in, b_in, conv_w, conv_b, conv_ln_g, conv_ln_b, w_conv_out, rel_bias,
           w_attn_out, w_o, b_o, out_ln_g, out_ln_b):
    bsz, seq, d_model = x.shape
    assert d_model == D_MODEL and seq % TILE == 0 and w_in.shape == (D_MODEL, D_IN)
    f32, bf16 = jnp.float32, jnp.bfloat16

    def row(v):
        return v.astype(f32).reshape(1, -1)

    def full(shape):
        return pl.BlockSpec(shape, lambda b, s: (0,) * len(shape))

    operands = [
        (x, pl.BlockSpec((None, TILE, D_MODEL), lambda b, s: (b, s, 0))),
        (w_in.astype(bf16), full((D_MODEL, D_IN))),
        (row(b_in), full((1, D_IN))),
        (conv_w.astype(f32), full((CONV_WIDTH, D_CONV))),
        (row(conv_b), full((1, D_CONV))),
        (row(conv_ln_g), full((1, D_CONV))),
        (row(conv_ln_b), full((1, D_CONV))),
        (w_conv_out.astype(bf16), full((D_CONV, D_MODEL))),
        (_rel_bias_table(rel_bias), full((N_HEADS, GROUP, KWIN))),
        (w_attn_out.astype(bf16), full((D_ATTN, D_MODEL))),
        (w_o.astype(bf16), full((D_MODEL, D_MODEL))),
        (row(b_o), full((1, D_MODEL))),
        (row(out_ln_g), full((1, D_MODEL))),
        (row(out_ln_b), full((1, D_MODEL))),
    ]
    scratch = [
        pltpu.VMEM((TILE, D_MODEL), bf16),
        pltpu.VMEM((CONV_SLABS, U_ROWS, LANES), f32),
        pltpu.VMEM((CONV_SLABS, TILE, LANES), f32),
        pltpu.VMEM((CONV_SLABS, TILE, LANES), f32),
        pltpu.VMEM((TILE, D_MODEL), bf16),
        pltpu.VMEM((TILE, D_MODEL), bf16),
        pltpu.VMEM((TILE, D_ATTN), bf16),
        pltpu.VMEM((2 * D_ATTN, HIST + TILE), bf16),
        pltpu.VMEM((HIST + TILE, 2 * D_ATTN), bf16),
        pltpu.VMEM((TILE, D_ATTN), bf16),
        pltpu.VMEM((TILE, D_ATTN), bf16),
    ]
    return pl.pallas_call(
        _layer_kernel,
        grid=(bsz, seq // TILE),
        in_specs=[spec for _, spec in operands],
        out_specs=pl.BlockSpec((None, TILE, D_MODEL), lambda b, s: (b, s, 0)),
        out_shape=jax.ShapeDtypeStruct(x.shape, x.dtype),
        scratch_shapes=scratch,
        compiler_params=pltpu.CompilerParams(
            dimension_semantics=("arbitrary", "arbitrary"),
            vmem_limit_bytes=VMEM_LIMIT_BYTES),
        name="hybrid_layer",
    )(*[a for a, _ in operands])
```

```python
import functools
import math

import numpy as np
import jax
import jax.numpy as jnp
from jax import lax
from jax.experimental import pallas as pl
from jax.experimental.pallas import tpu as pltpu

D_MODEL = 1024
CHUNK = 64
LEFT_CHUNKS = 8
D_CONV = D_MODEL // 2
CONV_WIDTH = 31
N_HEADS = 8
HEAD_DIM = 64
D_ATTN = N_HEADS * HEAD_DIM
MAX_REL = 128
LN_EPS = 1e-5
DEPTH = 1
DEEPNORM_ALPHA = (2.0 * DEPTH) ** 0.25

C_VAL, C_GLU, C_GATE = 0, D_CONV, 2 * D_CONV
Q_OFF = 3 * D_CONV
K_OFF = Q_OFF + D_ATTN
V_OFF = K_OFF + D_ATTN
AG_OFF = V_OFF + D_ATTN
GC_OFF = AG_OFF + D_ATTN
GA_OFF = GC_OFF + D_MODEL
D_IN = GA_OFF + D_MODEL

LANES = 128
SUBLANES = 8
MXU_COLS = 256
TILE = 512
GROUP = 4 * CHUNK
HIST = LEFT_CHUNKS * CHUNK
KWIN = HIST + GROUP
CONV_PAD = 32
CONV_ROWS = 32
SCORES_AHEAD = 1
U_ROWS = CONV_PAD + TILE
CONV_SLABS = D_CONV // LANES
ROW_PHASES = 2
NEG = -1e30
LOG2E = math.log2(math.e)
VMEM_LIMIT_BYTES = 58 * 1024 * 1024

assert TILE == HIST and TILE % GROUP == 0 and CONV_PAD >= CONV_WIDTH - 1
assert CONV_PAD % SUBLANES == 0 and CONV_ROWS % (ROW_PHASES * SUBLANES) == 0


def _sigmoid(x):
    return jax.nn.sigmoid(x)


def _silu(x):
    return x * jax.nn.sigmoid(x)


def _layer_kernel(x_ref, w_in_ref, b_in_ref, conv_w_ref, conv_b_ref, cln_g_ref, cln_b_ref,
                  w_co_ref, bias_ref, w_ao_ref, w_o_ref, b_o_ref, oln_g_ref, oln_b_ref,
                  out_ref,
                  xb_s, u_hist, cg_s, a_conv, gc_s, ga_s, q_s, kt_hist, v_hist, ag_s, o_s):
    f32, bf16 = jnp.float32, jnp.bfloat16
    s_idx = pl.program_id(1)

    @pl.when(s_idx == 0)
    def _():
        u_hist[:, 0:CONV_PAD, :] = jnp.zeros((CONV_SLABS, CONV_PAD, LANES), f32)
        kt_hist[...] = jnp.zeros(kt_hist.shape, bf16)
        v_hist[0:HIST, :] = jnp.zeros((HIST, 2 * D_ATTN), bf16)

    xb_s[...] = x_ref[...].astype(bf16)

    def proj(lo, width):
        z = jnp.dot(xb_s[...], w_in_ref[:, lo:lo + width], preferred_element_type=f32)
        return z + b_in_ref[:, lo:lo + width]

    u = proj(C_VAL, D_CONV) * _sigmoid(proj(C_GLU, D_CONV))
    cg = _silu(proj(C_GATE, D_CONV))
    for c in range(CONV_SLABS):
        u_hist[c, CONV_PAD:U_ROWS, :] = u[:, LANES * c:LANES * (c + 1)]
        cg_s[c] = cg[:, LANES * c:LANES * (c + 1)]

    lane = lax.broadcasted_iota(jnp.int32, (TILE, LANES), 1)
    low = lane < HEAD_DIM

    def q_piece(c):
        q_s[:, c:c + MXU_COLS] = (
            proj(Q_OFF + c, MXU_COLS) * (LOG2E / math.sqrt(HEAD_DIM))).astype(bf16)

    def k_piece(c):
        kv = proj(K_OFF + c, MXU_COLS)
        for jj in range(MXU_COLS // LANES):
            pair_t = kv[:, LANES * jj:LANES * (jj + 1)].T.astype(bf16)
            h0 = 2 * (c // LANES + jj)
            kt_hist[LANES * h0:LANES * h0 + HEAD_DIM, HIST:HIST + TILE] = pair_t[0:HEAD_DIM]
            kt_hist[LANES * (h0 + 1) + HEAD_DIM:LANES * (h0 + 2), HIST:HIST + TILE] = (
                pair_t[HEAD_DIM:LANES])

    def v_piece(c):
        kv = proj(V_OFF + c, MXU_COLS)
        for jj in range(MXU_COLS // LANES):
            pair = kv[:, LANES * jj:LANES * (jj + 1)]
            base = 2 * (c + LANES * jj)
            v_hist[HIST:HIST + TILE, base:base + LANES] = jnp.where(low, pair, 1.0).astype(bf16)
            v_hist[HIST:HIST + TILE, base + LANES:base + 2 * LANES] = (
                jnp.where(low, 1.0, pair).astype(bf16))

    def gate_piece(off, dst, act, c):
        dst[:, c:c + MXU_COLS] = act(proj(off + c, MXU_COLS)).astype(bf16)

    part = functools.partial
    assert D_ATTN == 2 * MXU_COLS
    pieces = [part(q_piece, 0), part(k_piece, 0), part(v_piece, 0)]
    late_pieces = [part(gate_piece, AG_OFF, ag_s, _silu, 0),
                   part(q_piece, MXU_COLS), part(k_piece, MXU_COLS), part(v_piece, MXU_COLS),
                   part(gate_piece, AG_OFF, ag_s, _silu, MXU_COLS)]
    for c in range(0, D_MODEL, MXU_COLS):
        late_pieces.append(part(gate_piece, GC_OFF, gc_s, _sigmoid, c))
    for c in range(0, D_MODEL, MXU_COLS):
        late_pieces.append(part(gate_piece, GA_OFF, ga_s, _sigmoid, c))
    n_blocks = TILE // CONV_ROWS

    cln_g = cln_g_ref[...]
    cln_b = cln_b_ref[...]
    row0 = CONV_PAD - (CONV_WIDTH - 1)
    span = ROW_PHASES * SUBLANES

    def conv_block(rb):
        r0 = rb * CONV_ROWS
        sets = [(a, p) for a in range(CONV_ROWS // span) for p in range(ROW_PHASES)]

        def rows_of(a, p, shift=0):
            return pl.ds(shift + r0 + span * a + p, SUBLANES, stride=ROW_PHASES)

        acc = [[jnp.broadcast_to(conv_b_ref[:, LANES * c:LANES * (c + 1)], (SUBLANES, LANES))
                for c in range(CONV_SLABS)] for _ in sets]
        loaded = {}

        def taps(c, a, p, k):
            key = (c, a, p + k)
            if key not in loaded:
                loaded[key] = u_hist[c, rows_of(a, p, row0 + k), :]
            return loaded[key]

        for k in range(CONV_WIDTH):
            for c in range(CONV_SLABS):
                w_kc = conv_w_ref[k:k + 1, LANES * c:LANES * (c + 1)]
                for i, (a, p) in enumerate(sets):
                    acc[i][c] = acc[i][c] + w_kc * taps(c, a, p, k)
        conv = jnp.concatenate([jnp.concatenate(acc_i, axis=1) for acc_i in acc], axis=0)
        gate = jnp.concatenate(
            [jnp.concatenate([cg_s[c, rows_of(a, p), :] for c in range(CONV_SLABS)], axis=1)
             for a, p in sets], axis=0)
        mu = jnp.mean(conv, axis=-1, keepdims=True)
        cen = conv - mu
        var = jnp.mean(cen * cen, axis=-1, keepdims=True)
        y = cen * lax.rsqrt(var + LN_EPS) * cln_g + cln_b
        act = _silu(y) * gate
        for i, (a, p) in enumerate(sets):
            for c in range(CONV_SLABS):
                a_conv[c, rows_of(a, p), :] = act[SUBLANES * i:SUBLANES * (i + 1),
                                                  LANES * c:LANES * (c + 1)]

    for piece in pieces:
        piece()
    for rb in range(n_blocks):
        conv_block(rb)

    first = s_idx == 0
    off_first = jnp.where(first, jnp.float32(-NEG), jnp.float32(0.0))
    lane_g = lax.broadcasted_iota(jnp.int32, (GROUP, LANES), 1)
    low_g = lane_g < HEAD_DIM
    col_tiles = KWIN // LANES
    band_tiles = (LEFT_CHUNKS + 1) * CHUNK // LANES + 1
    zero_tile = jnp.zeros((CHUNK, LANES), bf16)

    def scores(g, h):
        qp = q_s[GROUP * g:GROUP * (g + 1), LANES * (h // 2):LANES * (h // 2 + 1)]
        kh_t = kt_hist[LANES * h:LANES * (h + 1), GROUP * g:GROUP * g + KWIN]
        return jnp.dot(qp, kh_t, preferred_element_type=f32)

    def head_output(g, h, s):
        n_pre = (HIST - GROUP * g) // GROUP
        offs = [off_first if t < n_pre else None for t in range(KWIN // GROUP)]
        e_rows = []
        for ci in range(GROUP // CHUNK):
            r = slice(CHUNK * ci, CHUNK * (ci + 1))
            first_tile = (CHUNK * ci) // LANES
            live = range(first_tile, first_tile + band_tiles)
            sc = {t: s[r, LANES * t:LANES * (t + 1)] + bias_ref[h, r, LANES * t:LANES * (t + 1)]
                  for t in live}
            tops = [sc[t] if offs[LANES * t // GROUP] is None
                    else sc[t] - offs[LANES * t // GROUP] for t in live]
            m = jnp.max(functools.reduce(jnp.maximum, tops), axis=-1, keepdims=True)
            e_tiles = []
            for t in range(col_tiles):
                if t in live:
                    off = offs[LANES * t // GROUP]
                    mt = m if off is None else m + off
                    e_tiles.append(jnp.exp2(sc[t] - mt).astype(bf16))
                else:
                    e_tiles.append(zero_tile)
            e_rows.append(jnp.concatenate(e_tiles, axis=1))
        e = jnp.concatenate(e_rows, axis=0)
        vh = v_hist[GROUP * g:GROUP * g + KWIN, LANES * h:LANES * (h + 1)]
        return jnp.dot(e, vh, preferred_element_type=f32)

    half = N_HEADS // 2
    heads = [(g, h) for h0 in (0, half) for g in range(TILE // GROUP) for h in range(h0, h0 + half)]
    assert len(late_pieces) <= len(heads)
    pending = [scores(*head) for head in heads[:SCORES_AHEAD]]
    o_heads = []
    for n, (g, h) in enumerate(heads):
        s = pending.pop(0)
        if n + SCORES_AHEAD < len(heads):
            pending.append(scores(*heads[n + SCORES_AHEAD]))
        if late_pieces:
            late_pieces.pop(0)()
        o_heads.append(head_output(g, h, s))
        if h % 2 == 1:
            rows = slice(GROUP * g, GROUP * (g + 1))
            pair = slice(LANES * (h // 2), LANES * (h // 2 + 1))
            num = jnp.where(low_g, o_heads[0], o_heads[1])
            den = jnp.where(low_g, pltpu.roll(o_heads[0], HEAD_DIM, 1),
                            pltpu.roll(o_heads[1], HEAD_DIM, 1))
            o_s[rows, pair] = (num / den * ag_s[rows, pair]).astype(bf16)
            o_heads = []

    assert not late_pieces
    attn_out = jnp.dot(o_s[...], w_ao_ref[...], preferred_element_type=f32)
    a_all = jnp.concatenate([a_conv[c] for c in range(CONV_SLABS)], axis=1).astype(bf16)
    conv_out = jnp.dot(a_all, w_co_ref[...], preferred_element_type=f32)
    for rows in (slice(0, TILE // 2), slice(TILE // 2, TILE)):
        h = (gc_s[rows, :] * conv_out[rows, :].astype(bf16)
             + ga_s[rows, :] * attn_out[rows, :].astype(bf16))
        y = jnp.dot(h, w_o_ref[...], preferred_element_type=f32) + b_o_ref[...]
        r = DEEPNORM_ALPHA * x_ref[rows, :] + y
        mu = jnp.mean(r, axis=-1, keepdims=True)
        cen = r - mu
        var = jnp.mean(cen * cen, axis=-1, keepdims=True)
        out_ref[rows, :] = cen * lax.rsqrt(var + LN_EPS) * oln_g_ref[...] + oln_b_ref[...]

    u_hist[:, 0:CONV_PAD, :] = u_hist[:, TILE:U_ROWS, :]
    kt_hist[:, 0:HIST] = kt_hist[:, TILE:TILE + HIST]
    v_hist[0:HIST, :] = v_hist[TILE:TILE + HIST, :]


def _rel_bias_table(rel_bias):
    n = GROUP + KWIN
    diag = np.arange(n)
    idx = np.clip(HIST + (GROUP - 1) - diag, -MAX_REL, MAX_REL) + MAX_REL
    n_far = int(np.sum(idx == 2 * MAX_REL)) - 1
    n_near = int(np.sum(idx == 0)) - 1
    rb = rel_bias.astype(jnp.float32)
    vec = jnp.concatenate([
        jnp.broadcast_to(rb[:, -1:], (N_HEADS, n_far)),
        rb[:, ::-1],
        jnp.broadcast_to(rb[:, :1], (N_HEADS, n_near))], axis=1)
    assert vec.shape == (N_HEADS, n)
    skew = jnp.tile(vec, (1, GROUP))[:, :GROUP * (n - 1)].reshape(N_HEADS, GROUP, n - 1)
    table = skew[:, :, GROUP - 1:GROUP - 1 + KWIN]
    r = np.arange(GROUP)[:, None]
    c = np.arange(KWIN)[None, :]
    band = c // CHUNK - r // CHUNK
    in_band = (band >= 0) & (band <= LEFT_CHUNKS)
    return jnp.where(jnp.asarray(in_band)[None], table * LOG2E, NEG)


def kernel(x, w_in, b_in, conv_w, conv_b, conv_ln_g, conv_ln_b, w_conv_out, rel_bias,
           w_attn_out, w_o, b_o, out_ln_g, out_ln_b):
    bsz, seq, d_model = x.shape
    assert d_model == D_MODEL and seq % TILE == 0 and w_in.shape == (D_MODEL, D_IN)
    f32, bf16 = jnp.float32, jnp.bfloat16

    def row(v):
        return v.astype(f32).reshape(1, -1)

    def full(shape):
        return pl.BlockSpec(shape, lambda b, s: (0,) * len(shape))

    operands = [
        (x, pl.BlockSpec((None, TILE, D_MODEL), lambda b, s: (b, s, 0))),
        (w_in.astype(bf16), full((D_MODEL, D_IN))),
        (row(b_in), full((1, D_IN))),
        (conv_w.astype(f32), full((CONV_WIDTH, D_CONV))),
        (row(conv_b), full((1, D_CONV))),
        (row(conv_ln_g), full((1, D_CONV))),
        (row(conv_ln_b), full((1, D_CONV))),
        (w_conv_out.astype(bf16), full((D_CONV, D_MODEL))),
        (_rel_bias_table(rel_bias), full((N_HEADS, GROUP, KWIN))),
        (w_attn_out.astype(bf16), full((D_ATTN, D_MODEL))),
        (w_o.astype(bf16), full((D_MODEL, D_MODEL))),
        (row(b_o), full((1, D_MODEL))),
        (row(out_ln_g), full((1, D_MODEL))),
        (row(out_ln_b), full((1, D_MODEL))),
    ]
    scratch = [
        pltpu.VMEM((TILE, D_MODEL), bf16),
        pltpu.VMEM((CONV_SLABS, U_ROWS, LANES), f32),
        pltpu.VMEM((CONV_SLABS, TILE, LANES), f32),
        pltpu.VMEM((CONV_SLABS, TILE, LANES), f32),
        pltpu.VMEM((TILE, D_MODEL), bf16),
        pltpu.VMEM((TILE, D_MODEL), bf16),
        pltpu.VMEM((TILE, D_ATTN), bf16),
        pltpu.VMEM((2 * D_ATTN, HIST + TILE), bf16),
        pltpu.VMEM((HIST + TILE, 2 * D_ATTN), bf16),
        pltpu.VMEM((TILE, D_ATTN), bf16),
        pltpu.VMEM((TILE, D_ATTN), bf16),
    ]
    return pl.pallas_call(
        _layer_kernel,
        grid=(bsz, seq // TILE),
        in_specs=[spec for _, spec in operands],
        out_specs=pl.BlockSpec((None, TILE, D_MODEL), lambda b, s: (b, s, 0)),
        out_shape=jax.ShapeDtypeStruct(x.shape, x.dtype),
        scratch_shapes=scratch,
        compiler_params=pltpu.CompilerParams(
            dimension_semantics=("arbitrary", "arbitrary"),
            vmem_limit_bytes=VMEM_LIMIT_BYTES),
        name="hybrid_layer",
    )(*[a for a, _ in operands])
```

```python
import functools
import math

import numpy as np
import jax
import jax.numpy as jnp
from jax import lax
from jax.experimental import pallas as pl
from jax.experimental.pallas import tpu as pltpu

D_MODEL = 1024
CHUNK = 64
LEFT_CHUNKS = 8
D_CONV = D_MODEL // 2
CONV_WIDTH = 31
N_HEADS = 8
HEAD_DIM = 64
D_ATTN = N_HEADS * HEAD_DIM
MAX_REL = 128
LN_EPS = 1e-5
DEPTH = 1
DEEPNORM_ALPHA = (2.0 * DEPTH) ** 0.25

C_VAL, C_GLU, C_GATE = 0, D_CONV, 2 * D_CONV
Q_OFF = 3 * D_CONV
K_OFF = Q_OFF + D_ATTN
V_OFF = K_OFF + D_ATTN
AG_OFF = V_OFF + D_ATTN
GC_OFF = AG_OFF + D_ATTN
GA_OFF = GC_OFF + D_MODEL
D_IN = GA_OFF + D_MODEL

LANES = 128
SUBLANES = 8
MXU_COLS = 256
TILE = 512
GROUP = 4 * CHUNK
HIST = LEFT_CHUNKS * CHUNK
KWIN = HIST + GROUP
CONV_PAD = 32
CONV_ROWS = 32
SCORES_AHEAD = 1
U_ROWS = CONV_PAD + TILE
CONV_SLABS = D_CONV // LANES
ROW_PHASES = 2
NEG = -1e30
LOG2E = math.log2(math.e)
VMEM_LIMIT_BYTES = 58 * 1024 * 1024

assert TILE == HIST and TILE % GROUP == 0 and CONV_PAD >= CONV_WIDTH - 1
assert CONV_PAD % SUBLANES == 0 and CONV_ROWS % (ROW_PHASES * SUBLANES) == 0


def _sigmoid(x):
    return jax.nn.sigmoid(x)


def _silu(x):
    return x * jax.nn.sigmoid(x)


def _layer_kernel(x_ref, w_in_ref, b_in_ref, conv_w_ref, conv_b_ref, cln_g_ref, cln_b_ref,
                  w_co_ref, bias_ref, w_ao_ref, w_o_ref, b_o_ref, oln_g_ref, oln_b_ref,
                  out_ref,
                  xb_s, u_hist, cg_s, a_conv, gc_s, ga_s, q_s, kt_hist, v_hist, ag_s, o_s):
    f32, bf16 = jnp.float32, jnp.bfloat16
    s_idx = pl.program_id(1)

    @pl.when(s_idx == 0)
    def _():
        u_hist[:, 0:CONV_PAD, :] = jnp.zeros((CONV_SLABS, CONV_PAD, LANES), f32)
        kt_hist[...] = jnp.zeros(kt_hist.shape, bf16)
        v_hist[0:HIST, :] = jnp.zeros((HIST, 2 * D_ATTN), bf16)

    xb_s[...] = x_ref[...].astype(bf16)

    def proj(lo, width):
        z = jnp.dot(xb_s[...], w_in_ref[:, lo:lo + width], preferred_element_type=f32)
        return z + b_in_ref[:, lo:lo + width]

    u = proj(C_VAL, D_CONV) * _sigmoid(proj(C_GLU, D_CONV))
    cg = _silu(proj(C_GATE, D_CONV))
    for c in range(CONV_SLABS):
        u_hist[c, CONV_PAD:U_ROWS, :] = u[:, LANES * c:LANES * (c + 1)]
        cg_s[c] = cg[:, LANES * c:LANES * (c + 1)]

    lane = lax.broadcasted_iota(jnp.int32, (TILE, LANES), 1)
    low = lane < HEAD_DIM

    def q_piece(c):
        q_s[:, c:c + MXU_COLS] = (
            proj(Q_OFF + c, MXU_COLS) * (LOG2E / math.sqrt(HEAD_DIM))).astype(bf16)

    def k_piece(c):
        kv = proj(K_OFF + c, MXU_COLS)
        for jj in range(MXU_COLS // LANES):
            pair_t = kv[:, LANES * jj:LANES * (jj + 1)].T.astype(bf16)
            h0 = 2 * (c // LANES + jj)
            kt_hist[LANES * h0:LANES * h0 + HEAD_DIM, HIST:HIST + TILE] = pair_t[0:HEAD_DIM]
            kt_hist[LANES * (h0 + 1) + HEAD_DIM:LANES * (h0 + 2), HIST:HIST + TILE] = (
                pair_t[HEAD_DIM:LANES])

    def v_piece(c):
        kv = proj(V_OFF + c, MXU_COLS)
        for jj in range(MXU_COLS // LANES):
            pair = kv[:, LANES * jj:LANES * (jj + 1)]
            base = 2 * (c + LANES * jj)
            v_hist[HIST:HIST + TILE, base:base + LANES] = jnp.where(low, pair, 1.0).astype(bf16)
            v_hist[HIST:HIST + TILE, base + LANES:base + 2 * LANES] = (
                jnp.where(low, 1.0, pair).astype(bf16))

    def gate_piece(off, dst, act, c):
        dst[:, c:c + MXU_COLS] = act(proj(off + c, MXU_COLS)).astype(bf16)

    pieces = []
    for c in range(0, D_ATTN, MXU_COLS):
        pieces.append(functools.partial(q_piece, c))
    for c in range(0, D_ATTN, MXU_COLS):
        pieces.append(functools.partial(k_piece, c))
    for c in range(0, D_ATTN, MXU_COLS):
        pieces.append(functools.partial(v_piece, c))
    for c in range(0, D_ATTN, MXU_COLS):
        pieces.append(functools.partial(gate_piece, AG_OFF, ag_s, _silu, c))
    late_pieces = []
    for c in range(0, D_MODEL, MXU_COLS):
        late_pieces.append(functools.partial(gate_piece, GC_OFF, gc_s, _sigmoid, c))
    for c in range(0, D_MODEL, MXU_COLS):
        late_pieces.append(functools.partial(gate_piece, GA_OFF, ga_s, _sigmoid, c))
    n_blocks = TILE // CONV_ROWS
    blocks_per_piece = n_blocks // len(pieces)
    assert blocks_per_piece * len(pieces) == n_blocks

    cln_g = cln_g_ref[...]
    cln_b = cln_b_ref[...]
    row0 = CONV_PAD - (CONV_WIDTH - 1)
    span = ROW_PHASES * SUBLANES

    def conv_block(rb):
        r0 = rb * CONV_ROWS
        sets = [(a, p) for a in range(CONV_ROWS // span) for p in range(ROW_PHASES)]

        def rows_of(a, p, shift=0):
            return pl.ds(shift + r0 + span * a + p, SUBLANES, stride=ROW_PHASES)

        acc = [[jnp.broadcast_to(conv_b_ref[:, LANES * c:LANES * (c + 1)], (SUBLANES, LANES))
                for c in range(CONV_SLABS)] for _ in sets]
        loaded = {}

        def taps(c, a, p, k):
            key = (c, a, p + k)
            if key not in loaded:
                loaded[key] = u_hist[c, rows_of(a, p, row0 + k), :]
            return loaded[key]

        for k in range(CONV_WIDTH):
            for c in range(CONV_SLABS):
                w_kc = conv_w_ref[k:k + 1, LANES * c:LANES * (c + 1)]
                for i, (a, p) in enumerate(sets):
                    acc[i][c] = acc[i][c] + w_kc * taps(c, a, p, k)
        conv = jnp.concatenate([jnp.concatenate(acc_i, axis=1) for acc_i in acc], axis=0)
        gate = jnp.concatenate(
            [jnp.concatenate([cg_s[c, rows_of(a, p), :] for c in range(CONV_SLABS)], axis=1)
             for a, p in sets], axis=0)
        mu = jnp.mean(conv, axis=-1, keepdims=True)
        cen = conv - mu
        var = jnp.mean(cen * cen, axis=-1, keepdims=True)
        y = cen * lax.rsqrt(var + LN_EPS) * cln_g + cln_b
        act = _silu(y) * gate
        for i, (a, p) in enumerate(sets):
            for c in range(CONV_SLABS):
                a_conv[c, rows_of(a, p), :] = act[SUBLANES * i:SUBLANES * (i + 1),
                                                  LANES * c:LANES * (c + 1)]

    for n, piece in enumerate(pieces):
        piece()
        for rb in range(blocks_per_piece * n, blocks_per_piece * (n + 1)):
            conv_block(rb)

    first = s_idx == 0
    off_first = jnp.where(first, jnp.float32(-NEG), jnp.float32(0.0))
    lane_g = lax.broadcasted_iota(jnp.int32, (GROUP, LANES), 1)
    low_g = lane_g < HEAD_DIM
    col_tiles = KWIN // LANES
    band_tiles = (LEFT_CHUNKS + 1) * CHUNK // LANES + 1
    zero_tile = jnp.zeros((CHUNK, LANES), bf16)

    def scores(g, h):
        qp = q_s[GROUP * g:GROUP * (g + 1), LANES * (h // 2):LANES * (h // 2 + 1)]
        kh_t = kt_hist[LANES * h:LANES * (h + 1), GROUP * g:GROUP * g + KWIN]
        return jnp.dot(qp, kh_t, preferred_element_type=f32)

    def head_output(g, h, s):
        n_pre = (HIST - GROUP * g) // GROUP
        offs = [off_first if t < n_pre else None for t in range(KWIN // GROUP)]
        e_rows = []
        for ci in range(GROUP // CHUNK):
            r = slice(CHUNK * ci, CHUNK * (ci + 1))
            first_tile = (CHUNK * ci) // LANES
            live = range(first_tile, first_tile + band_tiles)
            sc = {t: s[r, LANES * t:LANES * (t + 1)] + bias_ref[h, r, LANES * t:LANES * (t + 1)]
                  for t in live}
            tops = [sc[t] if offs[LANES * t // GROUP] is None
                    else sc[t] - offs[LANES * t // GROUP] for t in live]
            m = jnp.max(functools.reduce(jnp.maximum, tops), axis=-1, keepdims=True)
            e_tiles = []
            for t in range(col_tiles):
                if t in live:
                    off = offs[LANES * t // GROUP]
                    mt = m if off is None else m + off
                    e_tiles.append(jnp.exp2(sc[t] - mt).astype(bf16))
                else:
                    e_tiles.append(zero_tile)
            e_rows.append(jnp.concatenate(e_tiles, axis=1))
        e = jnp.concatenate(e_rows, axis=0)
        vh = v_hist[GROUP * g:GROUP * g + KWIN, LANES * h:LANES * (h + 1)]
        return jnp.dot(e, vh, preferred_element_type=f32)

    heads = [(g, h) for g in range(TILE // GROUP) for h in range(N_HEADS)]
    late_every = len(heads) // len(late_pieces)
    pending = [scores(*head) for head in heads[:SCORES_AHEAD]]
    o_heads = []
    for n, (g, h) in enumerate(heads):
        s = pending.pop(0)
        if n + SCORES_AHEAD < len(heads):
            pending.append(scores(*heads[n + SCORES_AHEAD]))
        if late_pieces and n % late_every == 0:
            late_pieces.pop(0)()
        o_heads.append(head_output(g, h, s))
        if h % 2 == 1:
            rows = slice(GROUP * g, GROUP * (g + 1))
            pair = slice(LANES * (h // 2), LANES * (h // 2 + 1))
            num = jnp.where(low_g, o_heads[0], o_heads[1])
            den = jnp.where(low_g, pltpu.roll(o_heads[0], HEAD_DIM, 1),
                            pltpu.roll(o_heads[1], HEAD_DIM, 1))
            o_s[rows, pair] = (num / den * ag_s[rows, pair]).astype(bf16)
            o_heads = []

    assert not late_pieces
    attn_out = jnp.dot(o_s[...], w_ao_ref[...], preferred_element_type=f32)
    a_all = jnp.concatenate([a_conv[c] for c in range(CONV_SLABS)], axis=1).astype(bf16)
    conv_out = jnp.dot(a_all, w_co_ref[...], preferred_element_type=f32)
    for rows in (slice(0, TILE // 2), slice(TILE // 2, TILE)):
        h = (gc_s[rows, :] * conv_out[rows, :].astype(bf16)
             + ga_s[rows, :] * attn_out[rows, :].astype(bf16))
        y = jnp.dot(h, w_o_ref[...], preferred_element_type=f32) + b_o_ref[...]
        r = DEEPNORM_ALPHA * x_ref[rows, :] + y
        mu = jnp.mean(r, axis=-1, keepdims=True)
        cen = r - mu
        var = jnp.mean(cen * cen, axis=-1, keepdims=True)
        out_ref[rows, :] = cen * lax.rsqrt(var + LN_EPS) * oln_g_ref[...] + oln_b_ref[...]

    u_hist[:, 0:CONV_PAD, :] = u_hist[:, TILE:U_ROWS, :]
    kt_hist[:, 0:HIST] = kt_hist[:, TILE:TILE + HIST]
    v_hist[0:HIST, :] = v_hist[TILE:TILE + HIST, :]


def _rel_bias_table(rel_bias):
    n = GROUP + KWIN
    diag = np.arange(n)
    idx = np.clip(HIST + (GROUP - 1) - diag, -MAX_REL, MAX_REL) + MAX_REL
    n_far = int(np.sum(idx == 2 * MAX_REL)) - 1
    n_near = int(np.sum(idx == 0)) - 1
    rb = rel_bias.astype(jnp.float32)
    vec = jnp.concatenate([
        jnp.broadcast_to(rb[:, -1:], (N_HEADS, n_far)),
        rb[:, ::-1],
        jnp.broadcast_to(rb[:, :1], (N_HEADS, n_near))], axis=1)
    assert vec.shape == (N_HEADS, n)
    skew = jnp.tile(vec, (1, GROUP))[:, :GROUP * (n - 1)].reshape(N_HEADS, GROUP, n - 1)
    table = skew[:, :, GROUP - 1:GROUP - 1 + KWIN]
    r = np.arange(GROUP)[:, None]
    c = np.arange(KWIN)[None, :]
    band = c // CHUNK - r // CHUNK
    in_band = (band >= 0) & (band <= LEFT_CHUNKS)
    return jnp.where(jnp.asarray(in_band)[None], table * LOG2E, NEG)


def kernel(x, w_in, b_in, conv_w, conv_b, conv_ln_g, conv_ln_b, w_conv_out, rel_bias,
           w_attn_out, w_o, b_o, out_ln_g, out_ln_b):
    bsz, seq, d_model = x.shape
    assert d_model == D_MODEL and seq % TILE == 0 and w_in.shape == (D_MODEL, D_IN)
    f32, bf16 = jnp.float32, jnp.bfloat16

    def row(v):
        return v.astype(f32).reshape(1, -1)

    def full(shape):
        return pl.BlockSpec(shape, lambda b, s: (0,) * len(shape))

    operands = [
        (x, pl.BlockSpec((None, TILE, D_MODEL), lambda b, s: (b, s, 0))),
        (w_in.astype(bf16), full((D_MODEL, D_IN))),
        (row(b_in), full((1, D_IN))),
        (conv_w.astype(f32), full((CONV_WIDTH, D_CONV))),
        (row(conv_b), full((1, D_CONV))),
        (row(conv_ln_g), full((1, D_CONV))),
        (row(conv_ln_b), full((1, D_CONV))),
        (w_conv_out.astype(bf16), full((D_CONV, D_MODEL))),
        (_rel_bias_table(rel_bias), full((N_HEADS, GROUP, KWIN))),
        (w_attn_out.astype(bf16), full((D_ATTN, D_MODEL))),
        (w_o.astype(bf16), full((D_MODEL, D_MODEL))),
        (row(b_o), full((1, D_MODEL))),
        (row(out_ln_g), full((1, D_MODEL))),
        (row(out_ln_b), full((1, D_MODEL))),
    ]
    scratch = [
        pltpu.VMEM((TILE, D_MODEL), bf16),
        pltpu.VMEM((CONV_SLABS, U_ROWS, LANES), f32),
        pltpu.VMEM((CONV_SLABS, TILE, LANES), f32),
        pltpu.VMEM((CONV_SLABS, TILE, LANES), f32),
        pltpu.VMEM((TILE, D_MODEL), bf16),
        pltpu.VMEM((TILE, D_MODEL), bf16),
        pltpu.VMEM((TILE, D_ATTN), bf16),
        pltpu.VMEM((2 * D_ATTN, HIST + TILE), bf16),
        pltpu.VMEM((HIST + TILE, 2 * D_ATTN), bf16),
        pltpu.VMEM((TILE, D_ATTN), bf16),
        pltpu.VMEM((TILE, D_ATTN), bf16),
    ]
    return pl.pallas_call(
        _layer_kernel,
        grid=(bsz, seq // TILE),
        in_specs=[spec for _, spec in operands],
        out_specs=pl.BlockSpec((None, TILE, D_MODEL), lambda b, s: (b, s, 0)),
        out_shape=jax.ShapeDtypeStruct(x.shape, x.dtype),
        scratch_shapes=scratch,
        compiler_params=pltpu.CompilerParams(
            dimension_semantics=("arbitrary", "arbitrary"),
            vmem_limit_bytes=VMEM_LIMIT_BYTES),
        name="hybrid_layer",
    )(*[a for a, _ in operands])
```
